```python
import jax, jax.numpy as jnp
from jax import lax
import numpy as np

D_MODEL = 1024
BATCH = 32
SEQ = 2048
DEPTH = 2
DEC_BATCH = 8
DEC_SEQ = 2048
PAST_LEN = 128

N_MIXERS = 2
D_INNER = D_MODEL
HEAD_DIM = 64
N_HEADS = D_INNER // HEAD_DIM
DECAY_LORA = max(32, int(round(1.8 * D_MODEL ** 0.5 / 32)) * 32)
ICLR_LORA = max(32, int(round(1.8 * D_MODEL ** 0.5 / 32)) * 32)
N_SHIFT_MIX = 7
LNX_EPS = 64e-5
RMS_EPS = 1e-6
GRID_W = 64
WIN_ROWS = 8
WIN_COLS = 16
Q_COL_BLOCK = 16
KEY_COL_SPAN = Q_COL_BLOCK + WIN_COLS
NEG_INF = -1e30

kernel_name = 'hybrid_rwkv7_natten2d_gated_encoder'


def _rms_norm(x, g):
    x32 = x.astype(jnp.float32)
    y = x32 * lax.rsqrt(jnp.mean(x32 * x32, axis=-1, keepdims=True) + RMS_EPS) * g.astype(jnp.float32)
    return y.astype(x.dtype)


def _wkv7_scan(r, decay, k, v, a, b, reverse):
    bsz, _, h, n = r.shape

    def step(s, inp):
        r_t, w_t, k_t, v_t, a_t, b_t = inp
        sa = jnp.einsum('bhvk,bhk->bhv', s, a_t)
        s = s * w_t[:, :, None, :] + sa[..., None] * b_t[:, :, None, :] + v_t[..., None] * k_t[:, :, None, :]
        return s, jnp.einsum('bhvk,bhk->bhv', s, r_t)

    xs = tuple(jnp.swapaxes(u, 0, 1) for u in (r, decay, k, v, a, b))
    s0 = jnp.zeros((bsz, h, n, n), jnp.float32)
    _, y = lax.scan(step, s0, xs, reverse=reverse)
    return jnp.swapaxes(y, 0, 1)


def _rwkv7_branch(xn, mu, w_r, w_k, w_v, w_z, w0, w1, w2, a0, a1, a2, k_k, k_a, r_k, lnx_w, lnx_b, w_o):
    f32 = jnp.float32
    bsz, t, _ = xn.shape
    heads = lambda u: u.reshape(bsz, t, N_HEADS, HEAD_DIM)
    x_prev = jnp.pad(xn[:, :-1], ((0, 0), (1, 0), (0, 0)))
    x_next = jnp.pad(xn[:, 1:], ((0, 0), (0, 1), (0, 0)))
    xx = 0.5 * (x_prev + x_next) - xn
    lerp = lambda j: xn + xx * mu[j]
    r = heads(lerp(0) @ w_r).astype(f32)
    k = heads(lerp(1) @ w_k).astype(f32)
    v = heads(lerp(2) @ w_v).astype(f32)
    z = xn @ w_z
    kk = k * k_k.astype(f32).reshape(N_HEADS, HEAD_DIM)
    kk = kk / jnp.maximum(jnp.sqrt(jnp.sum(kk * kk, axis=-1, keepdims=True)), 1e-12)
    k_a_h = k_a.astype(f32).reshape(N_HEADS, HEAD_DIM)
    r_k_h = r_k.astype(f32)
    ys = []
    bonus = []
    for d in range(2):
        xw = lerp(3 + d)
        xa = lerp(5 + d)
        w_log = -jax.nn.softplus(-(w0[d] + jnp.tanh(xw @ w1[d]) @ w2[d]).astype(f32)) - 0.5
        decay = heads(jnp.exp(-jnp.exp(w_log)))
        a = heads(jax.nn.sigmoid((a0[d] + (xa @ a1[d]) @ a2[d]).astype(f32)))
        k_d = k * (1.0 + (a - 1.0) * k_a_h)
        ys.append(_wkv7_scan(r, decay, k_d, v, -kk, kk * a, reverse=(d == 1)))
        bonus.append(jnp.sum(r * k_d * r_k_h, axis=-1, keepdims=True) * v)
    y = ys[0] + ys[1]
    mean = jnp.mean(y, axis=-1, keepdims=True)
    var = jnp.mean(jnp.square(y - mean), axis=-1, keepdims=True)
    y = (y - mean) * lax.rsqrt(var + LNX_EPS) * lnx_w.astype(f32).reshape(N_HEADS, HEAD_DIM) \
        + lnx_b.astype(f32).reshape(N_HEADS, HEAD_DIM)
    y = y + bonus[0] + bonus[1]
    out = y.reshape(bsz, t, D_INNER).astype(xn.dtype) * jax.nn.silu(z)
    return out @ w_o


def _column_geometry():
    n_cb = GRID_W // Q_COL_BLOCK
    q_cols = np.arange(GRID_W).reshape(n_cb, Q_COL_BLOCK)
    win_start = np.clip(q_cols - WIN_COLS // 2, 0, GRID_W - WIN_COLS)
    blk_start = np.clip(q_cols[:, 0] - WIN_COLS // 2, 0, GRID_W - KEY_COL_SPAN)
    key_cols = blk_start[:, None] + np.arange(KEY_COL_SPAN)
    kc = key_cols[:, None, :]
    valid = (kc >= win_start[..., None]) & (kc < win_start[..., None] + WIN_COLS)
    rel_idx = np.clip(kc - q_cols[..., None] + WIN_COLS - 1, 0, 2 * WIN_COLS - 2)
    return key_cols, valid, rel_idx


def _neighbourhood_attention(q, k, v, rpb):
    f32 = jnp.float32
    bsz, t, h, dh = q.shape
    rows = t // GRID_W
    win_r = min(WIN_ROWS, rows)
    n_cb = GRID_W // Q_COL_BLOCK
    key_cols, valid, rel_idx = _column_geometry()
    qg = q.reshape(bsz, rows, n_cb, Q_COL_BLOCK, h, dh)
    kb = k.reshape(bsz, rows, GRID_W, h, dh)[:, :, key_cols]
    vb = v.reshape(bsz, rows, GRID_W, h, dh)[:, :, key_cols]
    rpb_c = rpb[:, :, rel_idx]
    mask = jnp.asarray(valid)[None, None, :, :, None, :]

    def one_row(r):
        rs = jnp.clip(r - win_r // 2, 0, rows - win_r)
        qr = lax.dynamic_index_in_dim(qg, r, axis=1, keepdims=False)
        kr = lax.dynamic_slice_in_dim(kb, rs, win_r, axis=1)
        vr = lax.dynamic_slice_in_dim(vb, rs, win_r, axis=1)
        ridx = rs + jnp.arange(win_r, dtype=jnp.int32) - r + WIN_ROWS - 1
        bias = jnp.transpose(jnp.take(rpb_c, ridx, axis=1), (0, 2, 3, 1, 4))
        s = jnp.einsum('bnqhd,bsnkhd->bhnqsk', qr, kr, preferred_element_type=f32)
        s = jnp.where(mask, s + bias.astype(f32), NEG_INF)
        p = jax.nn.softmax(s.reshape(s.shape[:4] + (-1,)), axis=-1).reshape(s.shape)
        return jnp.einsum('bhnqsk,bsnkhd->bnqhd', p.astype(v.dtype), vr)

    o = lax.map(one_row, jnp.arange(rows, dtype=jnp.int32))
    return jnp.moveaxis(o, 0, 1).reshape(bsz, t, h, dh)


def _natten_branch(xn, w_in, b_in, rpb, w_o, b_o):
    bsz, t, _ = xn.shape
    proj = xn @ w_in + b_in
    q, k, v, z = jnp.split(proj, 4, axis=-1)
    heads = lambda u: u.reshape(bsz, t, N_HEADS, HEAD_DIM)
    o = _neighbourhood_attention(heads(q) * (HEAD_DIM ** -0.5), heads(k), heads(v), rpb)
    return (o.reshape(bsz, t, D_INNER) * jax.nn.silu(z)) @ w_o + b_o


def _trunk(x, pre_norm_g, post_norm_g, rwkv_params, na_params):
    for i in range(DEPTH):
        j = i // N_MIXERS
        xn = _rms_norm(x, pre_norm_g[i])
        if i % N_MIXERS == 0:
            h = _rwkv7_branch(xn, *[p[j] for p in rwkv_params])
        else:
            h = _natten_branch(xn, *[p[j] for p in na_params])
        x = x + _rms_norm(h, post_norm_g[i])
    return x


def setup_inputs(seed: int = 0) -> dict:
    key = jax.random.key(seed)
    ks = jax.random.split(key, 32)
    f32 = jnp.float32
    nrm = lambda kk, shape, scale: jax.random.normal(kk, shape, f32) * scale
    na = (DEPTH + 1) // 2
    nb = DEPTH // 2
    E, D = D_INNER, D_MODEL
    return {
        'x_prompt': nrm(ks[0], (BATCH, SEQ, D), 1.0),
        'x_sample': nrm(ks[1], (DEC_BATCH, DEC_SEQ, D), 1.0),
        'pre_norm_g': 1.0 + nrm(ks[2], (DEPTH, D), 0.05),
        'post_norm_g': 1.0 + nrm(ks[3], (DEPTH, D), 0.05),
        'rk_mu': jax.random.uniform(ks[4], (na, N_SHIFT_MIX, D), f32),
        'rk_w_r': nrm(ks[5], (na, D, E), D ** -0.5),
        'rk_w_k': nrm(ks[6], (na, D, E), D ** -0.5),
        'rk_w_v': nrm(ks[7], (na, D, E), D ** -0.5),
        'rk_w_z': nrm(ks[8], (na, D, E), D ** -0.5),
        'rk_w0': jax.random.uniform(ks[9], (na, 2, E), f32, minval=-6.5, maxval=-1.5),
        'rk_w1': nrm(ks[10], (na, 2, D, DECAY_LORA), D ** -0.5),
        'rk_w2': nrm(ks[11], (na, 2, DECAY_LORA, E), 0.1 * DECAY_LORA ** -0.5),
        'rk_a0': nrm(ks[12], (na, 2, E), 0.1),
        'rk_a1': nrm(ks[13], (na, 2, D, ICLR_LORA), D ** -0.5),
        'rk_a2': nrm(ks[14], (na, 2, ICLR_LORA, E), 0.1 * ICLR_LORA ** -0.5),
        'rk_k_k': 0.85 + nrm(ks[15], (na, E), 0.05),
        'rk_k_a': 1.0 + nrm(ks[16], (na, E), 0.05),
        'rk_r_k': nrm(ks[17], (na, N_HEADS, HEAD_DIM), 0.1),
        'rk_lnx_w': 1.0 + nrm(ks[18], (na, E), 0.05),
        'rk_lnx_b': nrm(ks[19], (na, E), 0.02),
        'rk_w_o': nrm(ks[20], (na, E, D), E ** -0.5),
        'na_w_in': nrm(ks[21], (nb, D, 4 * E), D ** -0.5),
        'na_b_in': nrm(ks[22], (nb, 4 * E), 0.02),
        'na_rpb': nrm(ks[23], (nb, N_HEADS, 2 * WIN_ROWS - 1, 2 * WIN_COLS - 1), 0.1),
        'na_w_o': nrm(ks[24], (nb, E, D), E ** -0.5),
        'na_b_o': nrm(ks[25], (nb, D), 0.02),
    }


def reference(x_prompt, x_sample, pre_norm_g, post_norm_g, rk_mu, rk_w_r, rk_w_k, rk_w_v, rk_w_z,
              rk_w0, rk_w1, rk_w2, rk_a0, rk_a1, rk_a2, rk_k_k, rk_k_a, rk_r_k, rk_lnx_w, rk_lnx_b,
              rk_w_o, na_w_in, na_b_in, na_rpb, na_w_o, na_b_o):
    rwkv_params = (rk_mu, rk_w_r, rk_w_k, rk_w_v, rk_w_z, rk_w0, rk_w1, rk_w2, rk_a0, rk_a1, rk_a2,
                   rk_k_k, rk_k_a, rk_r_k, rk_lnx_w, rk_lnx_b, rk_w_o)
    na_params = (na_w_in, na_b_in, na_rpb, na_w_o, na_b_o)
    y_prompt = _trunk(x_prompt, pre_norm_g, post_norm_g, rwkv_params, na_params)
    y_sample = _trunk(x_sample, pre_norm_g, post_norm_g, rwkv_params, na_params)
    return (y_prompt, y_sample)
```

```python
import functools

import jax
import jax.numpy as jnp
import numpy as np
from jax import lax
from jax.experimental import pallas as pl
from jax.experimental.pallas import tpu as pltpu

D_MODEL = 1024
N_HEADS = 16
HEAD_DIM = 64
LORA = 64
RMS_EPS = 1e-6
LNX_EPS = 64e-5
GRID_W = 64
WIN_ROWS = 8
WIN_COLS = 16
NEG_INF = -1e30

LANES_PER_GROUP = 256
HEADS_PER_GROUP = LANES_PER_GROUP // HEAD_DIM
N_GROUPS = D_MODEL // LANES_PER_GROUP
CHUNK = 64
TIME_TILE = 256
WKV_TILE = 256
Q_ROWS = 4
VMEM_LIMIT = 56 * 1024 * 1024

F32 = jnp.float32
BF16 = jnp.bfloat16


def _dot(a, b):
    return jnp.dot(a, b, preferred_element_type=F32)


def _dot_nt(a, b):
    return lax.dot_general(a, b, (((1,), (1,)), ((), ())), preferred_element_type=F32)


def _dot_tn(a, b):
    return lax.dot_general(a, b, (((0,), (0,)), ((), ())), preferred_element_type=F32)


def _split_dot(x, ones_b):
    hi = x.astype(BF16)
    lo = (x - hi.astype(F32)).astype(BF16)
    return _dot(hi, ones_b) + _dot(lo, ones_b)


def _head_sum(x, ones_b):
    outs = []
    for g in range(x.shape[-1] // LANES_PER_GROUP):
        sl = slice(g * LANES_PER_GROUP, (g + 1) * LANES_PER_GROUP)
        outs.append(_split_dot(x[:, sl], ones_b))
    return jnp.concatenate(outs, axis=-1) if len(outs) > 1 else outs[0]


def _rms(x, g):
    return x * lax.rsqrt(jnp.mean(x * x, axis=-1, keepdims=True) + RMS_EPS) * g


def _silu(z):
    return z * (1.0 / (1.0 + jnp.exp(-z)))


def _const_spec(shape):
    nd = len(shape)
    return pl.BlockSpec(shape, lambda *_: (0,) * nd, pipeline_mode=pl.Buffered(1))


def _rwkv_pre_kernel(x_ref, xp_ref, xq_ref, g_ref, mu_ref, wr_ref, wk_ref, wv_ref, wz_ref,
                     w0_ref, w1_ref, w2_ref, a0_ref, a1_ref, a2_ref, kkp_ref, ones_ref,
                     r_out, k_out, v_out, z_out, kk_out, as0_out, as1_out, lw0_out, lw1_out):
    i = pl.program_id(1)
    nt = pl.num_programs(1)
    tt = x_ref.shape[1]
    g = g_ref[...]
    xn = _rms(x_ref[0], g)
    prev_row = _rms(xp_ref[0], g)[7:8]
    next_row = _rms(xq_ref[0], g)[0:1]
    prev_row = jnp.where(i == 0, 0.0, prev_row)
    next_row = jnp.where(i == nt - 1, 0.0, next_row)
    row = lax.broadcasted_iota(jnp.int32, (tt, 1), 0)
    x_prev = jnp.where(row == 0, prev_row, pltpu.roll(xn, 1, axis=0))
    x_next = jnp.where(row == tt - 1, next_row, pltpu.roll(xn, tt - 1, axis=0))
    xx = 0.5 * (x_prev + x_next) - xn

    def lerp(j):
        return (xn + xx * mu_ref[j:j + 1, :]).astype(BF16)

    r_out[0] = _dot(lerp(0), wr_ref[...]).astype(BF16)
    k = _dot(lerp(1), wk_ref[...])
    k_out[0] = k.astype(BF16)
    v_out[0] = _dot(lerp(2), wv_ref[...]).astype(BF16)
    z_out[0] = _dot(xn.astype(BF16), wz_ref[...]).astype(BF16)

    kk = k * kkp_ref[...]
    ss = _head_sum(kk * kk, ones_ref[...])
    kk_out[0] = (kk / jnp.maximum(jnp.sqrt(ss), 1e-12)).astype(BF16)

    for d, (as_out, lw_out) in enumerate(((as0_out, lw0_out), (as1_out, lw1_out))):
        hw = jnp.tanh(_dot(lerp(3 + d), w1_ref[d]))
        wl = w0_ref[d:d + 1, :] + _dot(hw.astype(BF16), w2_ref[d])
        y = -wl
        sp = jnp.maximum(y, 0.0) + jnp.log1p(jnp.exp(-jnp.abs(y)))
        lw_out[0] = -jnp.exp(-sp - 0.5)
        ha = _dot(lerp(5 + d), a1_ref[d])
        al = a0_ref[d:d + 1, :] + _dot(ha.astype(BF16), a2_ref[d])
        as_out[0] = (1.0 / (1.0 + jnp.exp(-al))).astype(BF16)


def _rwkv_pre(x, g, mu, wr, wk, wv, wz, w0, w1, w2, a0, a1, a2, kkp, ones_b):
    bsz, t, d = x.shape
    tt = TIME_TILE
    nt = t // tt
    rb = tt // 8
    tile = pl.BlockSpec((1, tt, d), lambda b, i: (b, i, 0))
    prev = pl.BlockSpec((1, 8, d), lambda b, i: (b, jnp.maximum(i * rb - 1, 0), 0))
    nxt = pl.BlockSpec((1, 8, d), lambda b, i: (b, jnp.minimum((i + 1) * rb, t // 8 - 1), 0))
    ins = [tile, prev, nxt] + [_const_spec(a.shape) for a in
                               (g, mu, wr, wk, wv, wz, w0, w1, w2, a0, a1, a2, kkp, ones_b)]
    bshape = jax.ShapeDtypeStruct((bsz, t, d), BF16)
    fshape = jax.ShapeDtypeStruct((bsz, t, d), F32)
    return pl.pallas_call(
        _rwkv_pre_kernel,
        grid=(bsz, nt),
        in_specs=ins,
        out_specs=[tile] * 9,
        out_shape=[bshape] * 7 + [fshape] * 2,
        compiler_params=pltpu.CompilerParams(
            dimension_semantics=("parallel", "parallel"), vmem_limit_bytes=VMEM_LIMIT),
        name="rwkv_pre",
    )(x, x, x, g, mu, wr, wk, wv, wz, w0, w1, w2, a0, a1, a2, kkp, ones_b)


def _wkv_kernel(r_ref, k_ref, v_ref, kk_ref, as_ref, lw_ref, ka_ref, y_ref, s_ref, *, reverse):
    tt = r_ref.shape[1]
    nch = tt // CHUNK
    gl = LANES_PER_GROUP

    @pl.when(pl.program_id(2) == 0)
    def _():
        s_ref[...] = jnp.zeros_like(s_ref)

    r2 = lax.broadcasted_iota(jnp.int32, (gl, gl), 0)
    c2 = lax.broadcasted_iota(jnp.int32, (gl, gl), 1)
    bdmask = (r2 // HEAD_DIM) == (c2 // HEAD_DIM)
    eye2 = r2 == c2
    tw = lax.broadcasted_iota(jnp.int32, (CHUNK, gl), 0)
    iw = lax.broadcasted_iota(jnp.int32, (CHUNK, gl), 1) % CHUNK
    if reverse:
        strict, incl = iw > tw, iw >= tw
    else:
        strict, incl = iw < tw, iw <= tw
    eye_w = iw == tw
    rt = lax.broadcasted_iota(jnp.int32, (tt, tt), 0)
    ct = lax.broadcasted_iota(jnp.int32, (tt, tt), 1)
    same = (rt // CHUNK) == (ct // CHUNK)
    tri = jnp.where(same & ((ct >= rt) if reverse else (ct <= rt)), 1.0, 0.0).astype(BF16)

    def bd(x):
        return jnp.where(bdmask, jnp.concatenate([x] * HEADS_PER_GROUP, axis=0), jnp.zeros((), x.dtype))

    r = r_ref[0].astype(F32)
    k = k_ref[0].astype(F32)
    kk = kk_ref[0].astype(F32)
    asg = as_ref[0].astype(F32)
    lw = lw_ref[0]
    kd = k * (1.0 + (asg - 1.0) * ka_ref[...])
    bvec = kk * asg

    c_in = _split_dot_left(tri, lw)
    c_ex = c_in - lw
    e_in = jnp.exp(c_in)
    e_neg = jnp.exp(-c_in)
    r_t = (r * e_in).astype(BF16)
    a_t = (-kk * jnp.exp(c_ex)).astype(BF16)
    k_t = (kd * e_neg).astype(BF16)
    b_t = (bvec * e_neg).astype(BF16)

    order = range(nch - 1, -1, -1) if reverse else range(nch)
    for ci in order:
        sl = slice(ci * CHUNK, (ci + 1) * CHUNK)
        last = ci * CHUNK if reverse else (ci + 1) * CHUNK - 1
        c_tot = c_in[last:last + 1, :]
        e_hat = jnp.exp(c_tot - c_in[sl])
        k_h = (kd[sl] * e_hat).astype(BF16)
        b_h = (bvec[sl] * e_hat).astype(BF16)
        vb = v_ref[0, sl, :]

        ar = jnp.concatenate([a_t[sl], r_t[sl]], axis=0)
        sb = _dot_nt(ar, bd(b_t[sl]))
        sk = _dot_nt(ar, bd(k_t[sl]))
        lab = jnp.where(strict, sb[:CHUNK], 0.0)
        lak = jnp.where(strict, sk[:CHUNK], 0.0).astype(BF16)
        arb = jnp.where(incl, sb[CHUNK:], 0.0).astype(BF16)
        ark = jnp.where(incl, sk[CHUNK:], 0.0).astype(BF16)

        lb = lab.astype(BF16)
        p = _dot(lb, bd(lb))
        x = jnp.where(eye_w, 1.0, 0.0) + lab
        n_lvl = int(np.log2(CHUNK)) - 1
        for lvl in range(n_lvl):
            pb = p.astype(BF16)
            bdp = bd(pb)
            if lvl < n_lvl - 1:
                xp = _dot(jnp.concatenate([x.astype(BF16), pb], axis=0), bdp)
                x = x + xp[:CHUNK]
                p = xp[CHUNK:]
            else:
                x = x + _dot(x.astype(BF16), bdp)
        tb = x.astype(BF16)

        la = _dot(jnp.concatenate([lak, ark], axis=0), bd(vb))
        a_hat = _dot(tb, bd(a_t[sl])).astype(BF16)
        u_hat = _dot(tb, bd(la[:CHUNK].astype(BF16))).astype(BF16)
        r_hat = (r_t[sl].astype(F32) + _dot(arb, bd(a_hat))).astype(BF16)
        y_hat = _dot(arb, bd(u_hat)) + la[CHUNK:]
        pm = _dot_tn(b_h, a_hat)
        qm = _dot_tn(jnp.concatenate([b_h, k_h], axis=0), jnp.concatenate([u_hat, vb], axis=0))
        p_bd = (jnp.where(bdmask, pm, 0.0) + jnp.where(eye2, jnp.exp(c_tot), 0.0)).astype(BF16)
        q_bd = jnp.where(bdmask, qm, 0.0)

        s_b = s_ref[...].astype(BF16)
        ys = _dot(jnp.concatenate([p_bd, r_hat], axis=0), s_b)
        s_ref[...] = ys[:gl] + q_bd
        y_ref[0, sl, :] = ys[gl:] + y_hat


def _split_dot_left(ones_b, x):
    hi = x.astype(BF16)
    lo = (x - hi.astype(F32)).astype(BF16)
    return _dot(ones_b, hi) + _dot(ones_b, lo)


def _wkv(r, k, v, kk, asg, lw, ka, reverse):
    bsz, t, d = r.shape
    tt = WKV_TILE
    nt = t // tt
    gl = LANES_PER_GROUP
    if reverse:
        tile = pl.BlockSpec((1, tt, gl), lambda b, g, i: (b, nt - 1 - i, g))
    else:
        tile = pl.BlockSpec((1, tt, gl), lambda b, g, i: (b, i, g))
    ka_spec = pl.BlockSpec((1, gl), lambda b, g, i: (0, g))
    return pl.pallas_call(
        functools.partial(_wkv_kernel, reverse=reverse),
        grid=(bsz, d // gl, nt),
        in_specs=[tile] * 6 + [ka_spec],
        out_specs=tile,
        out_shape=jax.ShapeDtypeStruct((bsz, t, d), F32),
        scratch_shapes=[pltpu.VMEM((gl, gl), F32)],
        compiler_params=pltpu.CompilerParams(
            dimension_semantics=("parallel", "parallel", "arbitrary"), vmem_limit_bytes=VMEM_LIMIT),
        name="wkv_bwd" if reverse else "wkv_fwd",
    )(r, k, v, kk, asg, lw, ka)


def _rwkv_post_kernel(yf_ref, yb_ref, r_ref, k_ref, v_ref, as0_ref, as1_ref, z_ref, x_ref,
                      ka_ref, rk_ref, lnw_ref, lnb_ref, wo_ref, pg_ref, ng_ref, win_ref, bin_ref,
                      ones_ref, x1_out, q_out, k_out, v_out, z_out):
    ones_b = ones_ref[...]
    y = yf_ref[0] + yb_ref[0]
    mean = _head_sum(y, ones_b) * (1.0 / HEAD_DIM)
    yc = y - mean
    var = _head_sum(yc * yc, ones_b) * (1.0 / HEAD_DIM)
    yn = yc * lax.rsqrt(var + LNX_EPS) * lnw_ref[...] + lnb_ref[...]
    r = r_ref[0].astype(F32)
    k = k_ref[0].astype(F32)
    v = v_ref[0].astype(F32)
    ka = ka_ref[...]
    kd_sum = k * (1.0 + (as0_ref[0].astype(F32) - 1.0) * ka) + k * (1.0 + (as1_ref[0].astype(F32) - 1.0) * ka)
    bonus = _head_sum(r * kd_sum * rk_ref[...], ones_b) * v
    out = (yn + bonus) * _silu(z_ref[0].astype(F32))
    h = _dot(out.astype(BF16), wo_ref[...])
    x1 = x_ref[0] + _rms(h, pg_ref[...])
    x1_out[0] = x1
    xn = _rms(x1, ng_ref[...]).astype(BF16)
    e = D_MODEL
    for j, o_ref in enumerate((q_out, k_out, v_out, z_out)):
        pj = _dot(xn, win_ref[:, j * e:(j + 1) * e]) + bin_ref[:, j * e:(j + 1) * e]
        if j == 0:
            pj = pj * (HEAD_DIM ** -0.5)
        o_ref[0] = pj.astype(BF16)


def _rwkv_post(yf, yb, r, k, v, as0, as1, z, x, ka, rk, lnw, lnb, wo, pg, ng, win, b_in, ones_b):
    bsz, t, d = x.shape
    tt = TIME_TILE
    tile = pl.BlockSpec((1, tt, d), lambda b, i: (b, i, 0))
    consts = (ka, rk, lnw, lnb, wo, pg, ng, win, b_in, ones_b)
    bshape = jax.ShapeDtypeStruct((bsz, t, d), BF16)
    return pl.pallas_call(
        _rwkv_post_kernel,
        grid=(bsz, t // tt),
        in_specs=[tile] * 9 + [_const_spec(a.shape) for a in consts],
        out_specs=[tile] * 5,
        out_shape=[jax.ShapeDtypeStruct((bsz, t, d), F32)] + [bshape] * 4,
        compiler_params=pltpu.CompilerParams(
            dimension_semantics=("parallel", "parallel"), vmem_limit_bytes=VMEM_LIMIT),
        name="rwkv_post",
    )(yf, yb, r, k, v, as0, as1, z, x, *consts)


def _natten_kernel(q_ref, k_ref, v_ref, z_ref, x_ref, bias_ref, wo_ref, bo_ref, pg_ref,
                   out_ref, o_scr, *, rows):
    step = pl.program_id(1)
    gl = LANES_PER_GROUP
    kwin = WIN_ROWS * GRID_W
    r2 = lax.broadcasted_iota(jnp.int32, (gl, gl), 0)
    c2 = lax.broadcasted_iota(jnp.int32, (gl, gl), 1)
    bdmask = (r2 // HEAD_DIM) == (c2 // HEAD_DIM)

    for j in range(Q_ROWS):
        row = step * Q_ROWS + j
        rs = jnp.clip(row - WIN_ROWS // 2, 0, rows - WIN_ROWS)
        dr0 = rs - row + WIN_ROWS - 1
        k0 = pl.multiple_of(rs * GRID_W, GRID_W)
        qs = slice(j * GRID_W, (j + 1) * GRID_W)
        for g in range(N_GROUPS):
            gs = slice(g * gl, (g + 1) * gl)
            qb = q_ref[0, qs, gs]
            qst = jnp.where(bdmask, jnp.concatenate([qb] * HEADS_PER_GROUP, axis=0), jnp.zeros((), BF16))
            kw = k_ref[0, pl.ds(k0, kwin), gs]
            vw = v_ref[0, pl.ds(k0, kwin), gs]
            s = _dot_nt(qst, kw)
            bias = jnp.concatenate([bias_ref[g, dr0 + 2 * p] for p in range(WIN_ROWS // 2)], axis=-1)
            s = s + bias
            m = jnp.max(s, axis=-1, keepdims=True)
            p = jnp.exp(s - m)
            l = jnp.sum(p, axis=-1, keepdims=True)
            pv = _dot(p.astype(BF16), vw) * (1.0 / l)
            pv = jnp.where(bdmask, pv, 0.0)
            o = pv[0:GRID_W]
            for h in range(1, HEADS_PER_GROUP):
                o = o + pv[h * GRID_W:(h + 1) * GRID_W]
            o_scr[qs, gs] = o

    gated = o_scr[...] * _silu(z_ref[0].astype(F32))
    h = _dot(gated.astype(BF16), wo_ref[...]) + bo_ref[...]
    out_ref[0] = x_ref[0] + _rms(h, pg_ref[...])


def _natten(q, k, v, z, x1, bias_tab, wo, bo, pg):
    bsz, t, d = x1.shape
    rows = t // GRID_W
    tq = Q_ROWS * GRID_W
    tile = pl.BlockSpec((1, tq, d), lambda b, i: (b, i, 0))
    full = pl.BlockSpec((1, t, d), lambda b, i: (b, 0, 0))
    consts = (bias_tab, wo, bo, pg)
    return pl.pallas_call(
        functools.partial(_natten_kernel, rows=rows),
        grid=(bsz, rows // Q_ROWS),
        in_specs=[tile, full, full, tile, tile] + [_const_spec(a.shape) for a in consts],
        out_specs=tile,
        out_shape=jax.ShapeDtypeStruct((bsz, t, d), F32),
        scratch_shapes=[pltpu.VMEM((tq, d), F32)],
        compiler_params=pltpu.CompilerParams(
            dimension_semantics=("parallel", "arbitrary"), vmem_limit_bytes=VMEM_LIMIT),
        name="natten",
    )(q, k, v, z, x1, *consts)


def _bias_table(rpb):
    qc = np.arange(GRID_W)[:, None]
    kc = np.arange(GRID_W)[None, :]
    ws = np.clip(qc - WIN_COLS // 2, 0, GRID_W - WIN_COLS)
    valid = (kc >= ws) & (kc < ws + WIN_COLS)
    rel = np.clip(kc - qc + WIN_COLS - 1, 0, 2 * WIN_COLS - 2)
    tab = jnp.where(jnp.asarray(valid)[None, None], rpb[:, :, rel], NEG_INF)
    two = jnp.concatenate([tab[:, :-1], tab[:, 1:]], axis=-1)
    two = two.reshape(N_GROUPS, HEADS_PER_GROUP, 2 * WIN_ROWS - 2, GRID_W, 2 * GRID_W)
    return jnp.transpose(two, (0, 2, 1, 3, 4)).reshape(
        N_GROUPS, 2 * WIN_ROWS - 2, HEADS_PER_GROUP * GRID_W, 2 * GRID_W).astype(F32)


def _trunk(x, p):
    row = lambda a: a.reshape(1, -1)
    r, k, v, z, kk, as0, as1, lw0, lw1 = _rwkv_pre(
        x, row(p["pre_g"][0]), p["mu"], p["w_r"], p["w_k"], p["w_v"], p["w_z"], p["w0"], p["w1"], p["w2"],
        p["a0"], p["a1"], p["a2"], row(p["k_k"]), p["ones"])
    ka = row(p["k_a"])
    yf = _wkv(r, k, v, kk, as0, lw0, ka, reverse=False)
    yb = _wkv(r, k, v, kk, as1, lw1, ka, reverse=True)
    x1, q, k2, v2, z2 = _rwkv_post(
        yf, yb, r, k, v, as0, as1, z, x, ka, row(p["r_k"]), row(p["lnx_w"]), row(p["lnx_b"]), p["rk_w_o"],
        row(p["post_g"][0]), row(p["pre_g"][1]), p["w_in"], row(p["b_in"]), p["ones"])
    return _natten(q, k2, v2, z2, x1, p["bias_tab"], p["na_w_o"], row(p["b_o"]), row(p["post_g"][1]))


def kernel(x_prompt, x_sample, pre_norm_g, post_norm_g, rk_mu, rk_w_r, rk_w_k, rk_w_v, rk_w_z, rk_w0, rk_w1, rk_w2, rk_a0, rk_a1, rk_a2, rk_k_k, rk_k_a, rk_r_k, rk_lnx_w, rk_lnx_b, rk_w_o, na_w_in, na_b_in, na_rpb, na_w_o, na_b_o):
    bf = lambda a: a.astype(BF16)
    hd = np.arange(LANES_PER_GROUP) // HEAD_DIM
    p = dict(
        pre_g=pre_norm_g, post_g=post_norm_g, mu=rk_mu[0],
        w_r=bf(rk_w_r[0]), w_k=bf(rk_w_k[0]), w_v=bf(rk_w_v[0]), w_z=bf(rk_w_z[0]),
        w0=rk_w0[0], w1=bf(rk_w1[0]), w2=bf(rk_w2[0]), a0=rk_a0[0], a1=bf(rk_a1[0]), a2=bf(rk_a2[0]),
        k_k=rk_k_k[0], k_a=rk_k_a[0], r_k=rk_r_k[0], lnx_w=rk_lnx_w[0], lnx_b=rk_lnx_b[0],
        rk_w_o=bf(rk_w_o[0]), w_in=bf(na_w_in[0]), b_in=na_b_in[0], na_w_o=bf(na_w_o[0]), b_o=na_b_o[0],
        bias_tab=_bias_table(na_rpb[0]),
        ones=jnp.asarray(hd[:, None] == hd[None, :], dtype=BF16),
    )
    return (_trunk(x_prompt, p), _trunk(x_sample, p))
```

```python
import functools

import jax
import jax.numpy as jnp
import numpy as np
from jax import lax
from jax.experimental import pallas as pl
from jax.experimental.pallas import tpu as pltpu

D_MODEL = 1024
N_HEADS = 16
HEAD_DIM = 64
LORA = 64
RMS_EPS = 1e-6
LNX_EPS = 64e-5
GRID_W = 64
WIN_ROWS = 8
WIN_COLS = 16
NEG_INF = -1e30

LANES_PER_GROUP = 256
HEADS_PER_GROUP = LANES_PER_GROUP // HEAD_DIM
N_GROUPS = D_MODEL // LANES_PER_GROUP
CHUNK = 64
TIME_TILE = 256
WKV_TILE = 512
Q_ROWS = 4
VMEM_LIMIT = 56 * 1024 * 1024

F32 = jnp.float32
BF16 = jnp.bfloat16


def _dot(a, b):
    return jnp.dot(a, b, preferred_element_type=F32)


def _dot_nt(a, b):
    return lax.dot_general(a, b, (((1,), (1,)), ((), ())), preferred_element_type=F32)


def _dot_tn(a, b):
    return lax.dot_general(a, b, (((0,), (0,)), ((), ())), preferred_element_type=F32)


def _split_dot(x, ones_b):
    hi = x.astype(BF16)
    lo = (x - hi.astype(F32)).astype(BF16)
    return _dot(hi, ones_b) + _dot(lo, ones_b)


def _head_sum(x, ones_b):
    outs = []
    for g in range(x.shape[-1] // LANES_PER_GROUP):
        sl = slice(g * LANES_PER_GROUP, (g + 1) * LANES_PER_GROUP)
        outs.append(_split_dot(x[:, sl], ones_b))
    return jnp.concatenate(outs, axis=-1) if len(outs) > 1 else outs[0]


def _rms(x, g):
    return x * lax.rsqrt(jnp.mean(x * x, axis=-1, keepdims=True) + RMS_EPS) * g


def _silu(z):
    return z * (1.0 / (1.0 + jnp.exp(-z)))


def _const_spec(shape):
    nd = len(shape)
    return pl.BlockSpec(shape, lambda *_: (0,) * nd, pipeline_mode=pl.Buffered(1))


def _rwkv_pre_kernel(x_ref, xp_ref, xq_ref, g_ref, mu_ref, wr_ref, wk_ref, wv_ref, wz_ref,
                     w0_ref, w1_ref, w2_ref, a0_ref, a1_ref, a2_ref, kkp_ref, ones_ref,
                     r_out, k_out, v_out, z_out, kk_out, as0_out, as1_out, lw0_out, lw1_out):
    i = pl.program_id(1)
    nt = pl.num_programs(1)
    tt = x_ref.shape[1]
    g = g_ref[...]
    xn = _rms(x_ref[0], g)
    prev_row = _rms(xp_ref[0], g)[7:8]
    next_row = _rms(xq_ref[0], g)[0:1]
    prev_row = jnp.where(i == 0, 0.0, prev_row)
    next_row = jnp.where(i == nt - 1, 0.0, next_row)
    row = lax.broadcasted_iota(jnp.int32, (tt, 1), 0)
    x_prev = jnp.where(row == 0, prev_row, pltpu.roll(xn, 1, axis=0))
    x_next = jnp.where(row == tt - 1, next_row, pltpu.roll(xn, tt - 1, axis=0))
    xx = 0.5 * (x_prev + x_next) - xn

    def lerp(j):
        return (xn + xx * mu_ref[j:j + 1, :]).astype(BF16)

    r_out[0] = _dot(lerp(0), wr_ref[...]).astype(BF16)
    k = _dot(lerp(1), wk_ref[...])
    k_out[0] = k.astype(BF16)
    v_out[0] = _dot(lerp(2), wv_ref[...]).astype(BF16)
    z_out[0] = _dot(xn.astype(BF16), wz_ref[...]).astype(BF16)

    kk = k * kkp_ref[...]
    ss = _head_sum(kk * kk, ones_ref[...])
    kk_out[0] = (kk / jnp.maximum(jnp.sqrt(ss), 1e-12)).astype(BF16)

    for d, (as_out, lw_out) in enumerate(((as0_out, lw0_out), (as1_out, lw1_out))):
        hw = jnp.tanh(_dot(lerp(3 + d), w1_ref[d]))
        wl = w0_ref[d:d + 1, :] + _dot(hw.astype(BF16), w2_ref[d])
        y = -wl
        sp = jnp.maximum(y, 0.0) + jnp.log1p(jnp.exp(-jnp.abs(y)))
        lw_out[0] = -jnp.exp(-sp - 0.5)
        ha = _dot(lerp(5 + d), a1_ref[d])
        al = a0_ref[d:d + 1, :] + _dot(ha.astype(BF16), a2_ref[d])
        as_out[0] = (1.0 / (1.0 + jnp.exp(-al))).astype(BF16)


def _rwkv_pre(x, g, mu, wr, wk, wv, wz, w0, w1, w2, a0, a1, a2, kkp, ones_b):
    bsz, t, d = x.shape
    tt = TIME_TILE
    nt = t // tt
    rb = tt // 8
    tile = pl.BlockSpec((1, tt, d), lambda b, i: (b, i, 0))
    prev = pl.BlockSpec((1, 8, d), lambda b, i: (b, jnp.maximum(i * rb - 1, 0), 0))
    nxt = pl.BlockSpec((1, 8, d), lambda b, i: (b, jnp.minimum((i + 1) * rb, t // 8 - 1), 0))
    ins = [tile, prev, nxt] + [_const_spec(a.shape) for a in
                               (g, mu, wr, wk, wv, wz, w0, w1, w2, a0, a1, a2, kkp, ones_b)]
    bshape = jax.ShapeDtypeStruct((bsz, t, d), BF16)
    fshape = jax.ShapeDtypeStruct((bsz, t, d), F32)
    return pl.pallas_call(
        _rwkv_pre_kernel,
        grid=(bsz, nt),
        in_specs=ins,
        out_specs=[tile] * 9,
        out_shape=[bshape] * 7 + [fshape] * 2,
        compiler_params=pltpu.CompilerParams(
            dimension_semantics=("parallel", "parallel"), vmem_limit_bytes=VMEM_LIMIT),
        name="rwkv_pre",
    )(x, x, x, g, mu, wr, wk, wv, wz, w0, w1, w2, a0, a1, a2, kkp, ones_b)


def _wkv_kernel(r_ref, k_ref, v_ref, kk_ref, as_ref, lw_ref, ka_ref, y_ref, s_ref, *, reverse):
    tt = r_ref.shape[1]
    nch = tt // CHUNK
    gl = LANES_PER_GROUP

    @pl.when(pl.program_id(2) == 0)
    def _():
        s_ref[...] = jnp.zeros_like(s_ref)

    r2 = lax.broadcasted_iota(jnp.int32, (gl, gl), 0)
    c2 = lax.broadcasted_iota(jnp.int32, (gl, gl), 1)
    bdmask = (r2 // HEAD_DIM) == (c2 // HEAD_DIM)
    eye2 = r2 == c2
    tw = lax.broadcasted_iota(jnp.int32, (CHUNK, gl), 0)
    iw = lax.broadcasted_iota(jnp.int32, (CHUNK, gl), 1) % CHUNK
    if reverse:
        strict, incl = iw > tw, iw >= tw
    else:
        strict, incl = iw < tw, iw <= tw
    eye_w = iw == tw
    rt = lax.broadcasted_iota(jnp.int32, (tt, tt), 0)
    ct = lax.broadcasted_iota(jnp.int32, (tt, tt), 1)
    same = (rt // CHUNK) == (ct // CHUNK)
    tri = jnp.where(same & ((ct >= rt) if reverse else (ct <= rt)), 1.0, 0.0).astype(BF16)

    def bd(x):
        return jnp.where(bdmask, jnp.concatenate([x] * HEADS_PER_GROUP, axis=0), jnp.zeros((), x.dtype))

    r = r_ref[0].astype(F32)
    k = k_ref[0].astype(F32)
    kk = kk_ref[0].astype(F32)
    asg = as_ref[0].astype(F32)
    lw = lw_ref[0]
    kd = k * (1.0 + (asg - 1.0) * ka_ref[...])
    bvec = kk * asg

    c_in = _split_dot_left(tri, lw)
    c_ex = c_in - lw
    e_in = jnp.exp(c_in)
    e_neg = jnp.exp(-c_in)
    r_t = (r * e_in).astype(BF16)
    a_t = (-kk * jnp.exp(c_ex)).astype(BF16)
    k_t = (kd * e_neg).astype(BF16)
    b_t = (bvec * e_neg).astype(BF16)

    chunks = list(range(nch))
    sls = [slice(ci * CHUNK, (ci + 1) * CHUNK) for ci in chunks]
    lasts = [ci * CHUNK if reverse else (ci + 1) * CHUNK - 1 for ci in chunks]
    c_tot = [c_in[l:l + 1, :] for l in lasts]
    e_hat = [jnp.exp(c_tot[ci] - c_in[sls[ci]]) for ci in chunks]
    k_h = [(kd[sls[ci]] * e_hat[ci]).astype(BF16) for ci in chunks]
    b_h = [(bvec[sls[ci]] * e_hat[ci]).astype(BF16) for ci in chunks]
    vb = [v_ref[0, sl, :] for sl in sls]

    ar = [jnp.concatenate([a_t[sl], r_t[sl]], axis=0) for sl in sls]
    sb = [_dot_nt(ar[ci], bd(b_t[sls[ci]])) for ci in chunks]
    sk = [_dot_nt(ar[ci], bd(k_t[sls[ci]])) for ci in chunks]
    lab = [jnp.where(strict, s[:CHUNK], 0.0) for s in sb]
    lak = [jnp.where(strict, s[:CHUNK], 0.0).astype(BF16) for s in sk]
    arb = [jnp.where(incl, s[CHUNK:], 0.0).astype(BF16) for s in sb]
    ark = [jnp.where(incl, s[CHUNK:], 0.0).astype(BF16) for s in sk]
    la = [_dot(jnp.concatenate([lak[ci], ark[ci]], axis=0), bd(vb[ci])) for ci in chunks]

    lb = [l.astype(BF16) for l in lab]
    p = [_dot(l, bd(l)) for l in lb]
    eye_f = jnp.where(eye_w, 1.0, 0.0)
    x = [eye_f + l for l in lab]
    n_lvl = int(np.log2(CHUNK)) - 1
    for lvl in range(n_lvl):
        pb = [q.astype(BF16) for q in p]
        if lvl < n_lvl - 1:
            xp = [_dot(jnp.concatenate([x[ci].astype(BF16), pb[ci]], axis=0), bd(pb[ci])) for ci in chunks]
            x = [x[ci] + xp[ci][:CHUNK] for ci in chunks]
            p = [q[CHUNK:] for q in xp]
        else:
            x = [x[ci] + _dot(x[ci].astype(BF16), bd(pb[ci])) for ci in chunks]
    tb = [q.astype(BF16) for q in x]

    a_hat = [_dot(tb[ci], bd(a_t[sls[ci]])).astype(BF16) for ci in chunks]
    u_hat = [_dot(tb[ci], bd(la[ci][:CHUNK].astype(BF16))).astype(BF16) for ci in chunks]
    r_hat = [(r_t[sls[ci]].astype(F32) + _dot(arb[ci], bd(a_hat[ci]))).astype(BF16) for ci in chunks]
    y_hat = [_dot(arb[ci], bd(u_hat[ci])) + la[ci][CHUNK:] for ci in chunks]
    pm = [_dot_tn(b_h[ci], a_hat[ci]) for ci in chunks]
    qm = [_dot_tn(jnp.concatenate([b_h[ci], k_h[ci]], axis=0), jnp.concatenate([u_hat[ci], vb[ci]], axis=0))
          for ci in chunks]
    p_bd = [(jnp.where(bdmask, pm[ci], 0.0) + jnp.where(eye2, jnp.exp(c_tot[ci]), 0.0)).astype(BF16)
            for ci in chunks]
    q_bd = [jnp.where(bdmask, q, 0.0) for q in qm]

    s = s_ref[...]
    for ci in (reversed(chunks) if reverse else chunks):
        ys = _dot(jnp.concatenate([p_bd[ci], r_hat[ci]], axis=0), s.astype(BF16))
        s = ys[:gl] + q_bd[ci]
        y_ref[0, sls[ci], :] = ys[gl:] + y_hat[ci]
    s_ref[...] = s


def _split_dot_left(ones_b, x):
    hi = x.astype(BF16)
    lo = (x - hi.astype(F32)).astype(BF16)
    return _dot(ones_b, hi) + _dot(ones_b, lo)


def _wkv(r, k, v, kk, asg, lw, ka, reverse):
    bsz, t, d = r.shape
    tt = WKV_TILE
    nt = t // tt
    gl = LANES_PER_GROUP
    if reverse:
        tile = pl.BlockSpec((1, tt, gl), lambda b, g, i: (b, nt - 1 - i, g))
    else:
        tile = pl.BlockSpec((1, tt, gl), lambda b, g, i: (b, i, g))
    ka_spec = pl.BlockSpec((1, gl), lambda b, g, i: (0, g))
    return pl.pallas_call(
        functools.partial(_wkv_kernel, reverse=reverse),
        grid=(bsz, d // gl, nt),
        in_specs=[tile] * 6 + [ka_spec],
        out_specs=tile,
        out_shape=jax.ShapeDtypeStruct((bsz, t, d), F32),
        scratch_shapes=[pltpu.VMEM((gl, gl), F32)],
        compiler_params=pltpu.CompilerParams(
            dimension_semantics=("parallel", "parallel", "arbitrary"), vmem_limit_bytes=VMEM_LIMIT),
        name="wkv_bwd" if reverse else "wkv_fwd",
    )(r, k, v, kk, asg, lw, ka)


def _rwkv_post_kernel(yf_ref, yb_ref, r_ref, k_ref, v_ref, as0_ref, as1_ref, z_ref, x_ref,
                      ka_ref, rk_ref, lnw_ref, lnb_ref, wo_ref, pg_ref, ng_ref, win_ref, bin_ref,
                      ones_ref, x1_out, q_out, k_out, v_out, z_out):
    ones_b = ones_ref[...]
    y = yf_ref[0] + yb_ref[0]
    mean = _head_sum(y, ones_b) * (1.0 / HEAD_DIM)
    yc = y - mean
    var = _head_sum(yc * yc, ones_b) * (1.0 / HEAD_DIM)
    yn = yc * lax.rsqrt(var + LNX_EPS) * lnw_ref[...] + lnb_ref[...]
    r = r_ref[0].astype(F32)
    k = k_ref[0].astype(F32)
    v = v_ref[0].astype(F32)
    ka = ka_ref[...]
    kd_sum = k * (1.0 + (as0_ref[0].astype(F32) - 1.0) * ka) + k * (1.0 + (as1_ref[0].astype(F32) - 1.0) * ka)
    bonus = _head_sum(r * kd_sum * rk_ref[...], ones_b) * v
    out = (yn + bonus) * _silu(z_ref[0].astype(F32))
    h = _dot(out.astype(BF16), wo_ref[...])
    x1 = x_ref[0] + _rms(h, pg_ref[...])
    x1_out[0] = x1
    xn = _rms(x1, ng_ref[...]).astype(BF16)
    e = D_MODEL
    for j, o_ref in enumerate((q_out, k_out, v_out, z_out)):
        pj = _dot(xn, win_ref[:, j * e:(j + 1) * e]) + bin_ref[:, j * e:(j + 1) * e]
        if j == 0:
            pj = pj * (HEAD_DIM ** -0.5)
        o_ref[0] = pj.astype(BF16)


def _rwkv_post(yf, yb, r, k, v, as0, as1, z, x, ka, rk, lnw, lnb, wo, pg, ng, win, b_in, ones_b):
    bsz, t, d = x.shape
    tt = TIME_TILE
    tile = pl.BlockSpec((1, tt, d), lambda b, i: (b, i, 0))
    consts = (ka, rk, lnw, lnb, wo, pg, ng, win, b_in, ones_b)
    bshape = jax.ShapeDtypeStruct((bsz, t, d), BF16)
    return pl.pallas_call(
        _rwkv_post_kernel,
        grid=(bsz, t // tt),
        in_specs=[tile] * 9 + [_const_spec(a.shape) for a in consts],
        out_specs=[tile] * 5,
        out_shape=[jax.ShapeDtypeStruct((bsz, t, d), F32)] + [bshape] * 4,
        compiler_params=pltpu.CompilerParams(
            dimension_semantics=("parallel", "parallel"), vmem_limit_bytes=VMEM_LIMIT),
        name="rwkv_post",
    )(yf, yb, r, k, v, as0, as1, z, x, *consts)


def _natten_kernel(q_ref, k_ref, v_ref, z_ref, x_ref, bias_ref, wo_ref, bo_ref, pg_ref,
                   out_ref, o_scr, *, rows):
    step = pl.program_id(1)
    gl = LANES_PER_GROUP
    kwin = WIN_ROWS * GRID_W
    r2 = lax.broadcasted_iota(jnp.int32, (gl, gl), 0)
    c2 = lax.broadcasted_iota(jnp.int32, (gl, gl), 1)
    bdmask = (r2 // HEAD_DIM) == (c2 // HEAD_DIM)

    for j in range(Q_ROWS):
        row = step * Q_ROWS + j
        rs = jnp.clip(row - WIN_ROWS // 2, 0, rows - WIN_ROWS)
        dr0 = rs - row + WIN_ROWS - 1
        k0 = pl.multiple_of(rs * GRID_W, GRID_W)
        qs = slice(j * GRID_W, (j + 1) * GRID_W)
        for g in range(N_GROUPS):
            gs = slice(g * gl, (g + 1) * gl)
            qb = q_ref[0, qs, gs]
            qst = jnp.where(bdmask, jnp.concatenate([qb] * HEADS_PER_GROUP, axis=0), jnp.zeros((), BF16))
            kw = k_ref[0, pl.ds(k0, kwin), gs]
            vw = v_ref[0, pl.ds(k0, kwin), gs]
            s = _dot_nt(qst, kw)
            bias = jnp.concatenate([bias_ref[g, dr0 + 2 * p] for p in range(WIN_ROWS // 2)], axis=-1)
            s = s + bias
            m = jnp.max(s, axis=-1, keepdims=True)
            p = jnp.exp(s - m)
            l = jnp.sum(p, axis=-1, keepdims=True)
            pv = _dot(p.astype(BF16), vw) * (1.0 / l)
            pv = jnp.where(bdmask, pv, 0.0)
            o = pv[0:GRID_W]
            for h in range(1, HEADS_PER_GROUP):
                o = o + pv[h * GRID_W:(h + 1) * GRID_W]
            o_scr[qs, gs] = o

    gated = o_scr[...] * _silu(z_ref[0].astype(F32))
    h = _dot(gated.astype(BF16), wo_ref[...]) + bo_ref[...]
    out_ref[0] = x_ref[0] + _rms(h, pg_ref[...])


def _natten(q, k, v, z, x1, bias_tab, wo, bo, pg):
    bsz, t, d = x1.shape
    rows = t // GRID_W
    tq = Q_ROWS * GRID_W
    tile = pl.BlockSpec((1, tq, d), lambda b, i: (b, i, 0))
    full = pl.BlockSpec((1, t, d), lambda b, i: (b, 0, 0))
    consts = (bias_tab, wo, bo, pg)
    return pl.pallas_call(
        functools.partial(_natten_kernel, rows=rows),
        grid=(bsz, rows // Q_ROWS),
        in_specs=[tile, full, full, tile, tile] + [_const_spec(a.shape) for a in consts],
        out_specs=tile,
        out_shape=jax.ShapeDtypeStruct((bsz, t, d), F32),
        scratch_shapes=[pltpu.VMEM((tq, d), F32)],
        compiler_params=pltpu.CompilerParams(
            dimension_semantics=("parallel", "arbitrary"), vmem_limit_bytes=VMEM_LIMIT),
        name="natten",
    )(q, k, v, z, x1, *consts)


def _bias_table(rpb):
    qc = np.arange(GRID_W)[:, None]
    kc = np.arange(GRID_W)[None, :]
    ws = np.clip(qc - WIN_COLS // 2, 0, GRID_W - WIN_COLS)
    valid = (kc >= ws) & (kc < ws + WIN_COLS)
    rel = np.clip(kc - qc + WIN_COLS - 1, 0, 2 * WIN_COLS - 2)
    tab = jnp.where(jnp.asarray(valid)[None, None], rpb[:, :, rel], NEG_INF)
    two = jnp.concatenate([tab[:, :-1], tab[:, 1:]], axis=-1)
    two = two.reshape(N_GROUPS, HEADS_PER_GROUP, 2 * WIN_ROWS - 2, GRID_W, 2 * GRID_W)
    return jnp.transpose(two, (0, 2, 1, 3, 4)).reshape(
        N_GROUPS, 2 * WIN_ROWS - 2, HEADS_PER_GROUP * GRID_W, 2 * GRID_W).astype(F32)


def _trunk(x, p):
    row = lambda a: a.reshape(1, -1)
    r, k, v, z, kk, as0, as1, lw0, lw1 = _rwkv_pre(
        x, row(p["pre_g"][0]), p["mu"], p["w_r"], p["w_k"], p["w_v"], p["w_z"], p["w0"], p["w1"], p["w2"],
        p["a0"], p["a1"], p["a2"], row(p["k_k"]), p["ones"])
    ka = row(p["k_a"])
    yf = _wkv(r, k, v, kk, as0, lw0, ka, reverse=False)
    yb = _wkv(r, k, v, kk, as1, lw1, ka, reverse=True)
    x1, q, k2, v2, z2 = _rwkv_post(
        yf, yb, r, k, v, as0, as1, z, x, ka, row(p["r_k"]), row(p["lnx_w"]), row(p["lnx_b"]), p["rk_w_o"],
        row(p["post_g"][0]), row(p["pre_g"][1]), p["w_in"], row(p["b_in"]), p["ones"])
    return _natten(q, k2, v2, z2, x1, p["bias_tab"], p["na_w_o"], row(p["b_o"]), row(p["post_g"][1]))


def kernel(x_prompt, x_sample, pre_norm_g, post_norm_g, rk_mu, rk_w_r, rk_w_k, rk_w_v, rk_w_z, rk_w0, rk_w1, rk_w2, rk_a0, rk_a1, rk_a2, rk_k_k, rk_k_a, rk_r_k, rk_lnx_w, rk_lnx_b, rk_w_o, na_w_in, na_b_in, na_rpb, na_w_o, na_b_o):
    bf = lambda a: a.astype(BF16)
    hd = np.arange(LANES_PER_GROUP) // HEAD_DIM
    p = dict(
        pre_g=pre_norm_g, post_g=post_norm_g, mu=rk_mu[0],
        w_r=bf(rk_w_r[0]), w_k=bf(rk_w_k[0]), w_v=bf(rk_w_v[0]), w_z=bf(rk_w_z[0]),
        w0=rk_w0[0], w1=bf(rk_w1[0]), w2=bf(rk_w2[0]), a0=rk_a0[0], a1=bf(rk_a1[0]), a2=bf(rk_a2[0]),
        k_k=rk_k_k[0], k_a=rk_k_a[0], r_k=rk_r_k[0], lnx_w=rk_lnx_w[0], lnx_b=rk_lnx_b[0],
        rk_w_o=bf(rk_w_o[0]), w_in=bf(na_w_in[0]), b_in=na_b_in[0], na_w_o=bf(na_w_o[0]), b_o=na_b_o[0],
        bias_tab=_bias_table(na_rpb[0]),
        ones=jnp.asarray(hd[:, None] == hd[None, :], dtype=BF16),
    )
    return (_trunk(x_prompt, p), _trunk(x_sample, p))
```

```python
import functools

import jax
import jax.numpy as jnp
import numpy as np
from jax import lax
from jax.experimental import pallas as pl
from jax.experimental.pallas import tpu as pltpu

D_MODEL = 1024
N_HEADS = 16
HEAD_DIM = 64
LORA = 64
RMS_EPS = 1e-6
LNX_EPS = 64e-5
GRID_W = 64
WIN_ROWS = 8
WIN_COLS = 16
NEG_INF = -1e30

LANE_TILE = 128
HEADS_PER_TILE = LANE_TILE // HEAD_DIM
LANES_PER_GROUP = 256
HEADS_PER_GROUP = LANES_PER_GROUP // HEAD_DIM
N_GROUPS = D_MODEL // LANES_PER_GROUP
CHUNK = 64
TIME_TILE = 256
WKV_TILE = 512
Q_ROWS = 4
VMEM_LIMIT = 56 * 1024 * 1024

F32 = jnp.float32
BF16 = jnp.bfloat16


def _dot(a, b):
    return jnp.dot(a, b, preferred_element_type=F32)


def _dot_nt(a, b):
    return lax.dot_general(a, b, (((1,), (1,)), ((), ())), preferred_element_type=F32)


def _dot_tn(a, b):
    return lax.dot_general(a, b, (((0,), (0,)), ((), ())), preferred_element_type=F32)


def _split_dot(x, ones_b):
    hi = x.astype(BF16)
    lo = (x - hi.astype(F32)).astype(BF16)
    return _dot(hi, ones_b) + _dot(lo, ones_b)


def _split_dot_left(ones_b, x):
    hi = x.astype(BF16)
    lo = (x - hi.astype(F32)).astype(BF16)
    return _dot(ones_b, hi) + _dot(ones_b, lo)


def _head_sum(x, ones_b):
    outs = []
    for g in range(x.shape[-1] // LANES_PER_GROUP):
        sl = slice(g * LANES_PER_GROUP, (g + 1) * LANES_PER_GROUP)
        outs.append(_split_dot(x[:, sl], ones_b))
    return jnp.concatenate(outs, axis=-1) if len(outs) > 1 else outs[0]


def _rms(x, g):
    return x * lax.rsqrt(jnp.mean(x * x, axis=-1, keepdims=True) + RMS_EPS) * g


def _silu(z):
    return z * (1.0 / (1.0 + jnp.exp(-z)))


def _const_spec(shape):
    nd = len(shape)
    return pl.BlockSpec(shape, lambda *_: (0,) * nd, pipeline_mode=pl.Buffered(1))


def _rwkv_pre_kernel(x_ref, xp_ref, xq_ref, g_ref, mu_ref, wr_ref, wk_ref, wv_ref, wz_ref,
                     w0_ref, w1_ref, w2_ref, a0_ref, a1_ref, a2_ref, kkp_ref, ones_ref,
                     r_out, k_out, v_out, z_out, kk_out, as0_out, as1_out, lw0_out, lw1_out):
    i = pl.program_id(1)
    nt = pl.num_programs(1)
    tt = x_ref.shape[1]
    g = g_ref[...]
    xn = _rms(x_ref[0], g)
    prev_row = _rms(xp_ref[0], g)[7:8]
    next_row = _rms(xq_ref[0], g)[0:1]
    prev_row = jnp.where(i == 0, 0.0, prev_row)
    next_row = jnp.where(i == nt - 1, 0.0, next_row)
    row = lax.broadcasted_iota(jnp.int32, (tt, 1), 0)
    x_prev = jnp.where(row == 0, prev_row, pltpu.roll(xn, 1, axis=0))
    x_next = jnp.where(row == tt - 1, next_row, pltpu.roll(xn, tt - 1, axis=0))
    xx = 0.5 * (x_prev + x_next) - xn

    def lerp(j):
        return (xn + xx * mu_ref[j:j + 1, :]).astype(BF16)

    r_out[0] = _dot(lerp(0), wr_ref[...]).astype(BF16)
    k = _dot(lerp(1), wk_ref[...])
    k_out[0] = k.astype(BF16)
    v_out[0] = _dot(lerp(2), wv_ref[...]).astype(BF16)
    z_out[0] = _dot(xn.astype(BF16), wz_ref[...]).astype(BF16)

    kk = k * kkp_ref[...]
    ss = _head_sum(kk * kk, ones_ref[...])
    kk_out[0] = (kk / jnp.maximum(jnp.sqrt(ss), 1e-12)).astype(BF16)

    for d, (as_out, lw_out) in enumerate(((as0_out, lw0_out), (as1_out, lw1_out))):
        hw = jnp.tanh(_dot(lerp(3 + d), w1_ref[d]))
        wl = w0_ref[d:d + 1, :] + _dot(hw.astype(BF16), w2_ref[d])
        lw_out[0] = (-float(np.exp(-0.5))) / (1.0 + jnp.exp(-wl))
        ha = _dot(lerp(5 + d), a1_ref[d])
        al = a0_ref[d:d + 1, :] + _dot(ha.astype(BF16), a2_ref[d])
        as_out[0] = (1.0 / (1.0 + jnp.exp(-al))).astype(BF16)


def _rwkv_pre(x, g, mu, wr, wk, wv, wz, w0, w1, w2, a0, a1, a2, kkp, ones_b):
    bsz, t, d = x.shape
    tt = TIME_TILE
    nt = t // tt
    rb = tt // 8
    tile = pl.BlockSpec((1, tt, d), lambda b, i: (b, i, 0))
    prev = pl.BlockSpec((1, 8, d), lambda b, i: (b, jnp.maximum(i * rb - 1, 0), 0))
    nxt = pl.BlockSpec((1, 8, d), lambda b, i: (b, jnp.minimum((i + 1) * rb, t // 8 - 1), 0))
    ins = [tile, prev, nxt] + [_const_spec(a.shape) for a in
                               (g, mu, wr, wk, wv, wz, w0, w1, w2, a0, a1, a2, kkp, ones_b)]
    bshape = jax.ShapeDtypeStruct((bsz, t, d), BF16)
    fshape = jax.ShapeDtypeStruct((bsz, t, d), F32)
    return pl.pallas_call(
        _rwkv_pre_kernel,
        grid=(bsz, nt),
        in_specs=ins,
        out_specs=[tile] * 9,
        out_shape=[bshape] * 7 + [fshape] * 2,
        compiler_params=pltpu.CompilerParams(
            dimension_semantics=("parallel", "parallel"), vmem_limit_bytes=VMEM_LIMIT),
        name="rwkv_pre",
    )(x, x, x, g, mu, wr, wk, wv, wz, w0, w1, w2, a0, a1, a2, kkp, ones_b)


def _head_lane_masks(rows, dtype):
    lane = lax.broadcasted_iota(jnp.int32, (rows, LANE_TILE), 1)
    return [jnp.where((lane // HEAD_DIM) == j, 1.0, 0.0).astype(dtype) for j in range(HEADS_PER_TILE)]


def _bd(x, masks):
    zero = jnp.zeros((x.shape[0], LANE_TILE), x.dtype)
    n_tiles = x.shape[1] // LANE_TILE
    blocks = []
    for h in range(HEADS_PER_GROUP):
        lt, j = divmod(h, HEADS_PER_TILE)
        piece = x[:, lt * LANE_TILE:(lt + 1) * LANE_TILE] * masks[j]
        blocks.append(jnp.concatenate([piece if t == lt else zero for t in range(n_tiles)], axis=1))
    return jnp.concatenate(blocks, axis=0)


def _block_transpose(x):
    tiles = []
    for lt in range(x.shape[1] // LANE_TILE):
        t = x[:, lt * LANE_TILE:(lt + 1) * LANE_TILE].T
        tiles.append(jnp.concatenate([t[j * HEAD_DIM:(j + 1) * HEAD_DIM] for j in range(HEADS_PER_TILE)], axis=1))
    return jnp.concatenate(tiles, axis=1)


def _wkv_kernel(rf_ref, kf_ref, vf_ref, kkf_ref, asf_ref, lwf_ref,
                rb_ref, kb_ref, vb_ref, kkb_ref, asb_ref, lwb_ref, ka_ref,
                yf_ref, yb_ref, sf_ref, sb_ref):
    tt = rf_ref.shape[1]
    nch = tt // CHUNK
    gl = LANES_PER_GROUP

    @pl.when(pl.program_id(2) == 0)
    def _():
        sf_ref[...] = jnp.zeros_like(sf_ref)
        sb_ref[...] = jnp.zeros_like(sb_ref)

    mb = _head_lane_masks(CHUNK, BF16)
    tw = lax.broadcasted_iota(jnp.int32, (CHUNK, gl), 0)
    iw = lax.broadcasted_iota(jnp.int32, (CHUNK, gl), 1) % CHUNK
    eye_f = jnp.where(iw == tw, 1.0, 0.0)
    rt = lax.broadcasted_iota(jnp.int32, (tt, tt), 0)
    ct = lax.broadcasted_iota(jnp.int32, (tt, tt), 1)
    same = (rt // CHUNK) == (ct // CHUNK)
    ka = ka_ref[...]

    dirs = []
    for reverse, refs in ((False, (rf_ref, kf_ref, vf_ref, kkf_ref, asf_ref, lwf_ref)),
                          (True, (rb_ref, kb_ref, vb_ref, kkb_ref, asb_ref, lwb_ref))):
        r_ref, k_ref, v_ref, kk_ref, as_ref, lw_ref = refs
        tri = jnp.where(same & ((ct >= rt) if reverse else (ct <= rt)), 1.0, 0.0).astype(BF16)
        kk = kk_ref[0].astype(F32)
        asg = as_ref[0].astype(F32)
        lw = lw_ref[0]
        kd = k_ref[0].astype(F32) * (1.0 + (asg - 1.0) * ka)
        bvec = kk * asg
        c_in = _split_dot_left(tri, lw)
        e_neg = jnp.exp(-c_in)
        dirs.append(dict(
            reverse=reverse, v_ref=v_ref, kd=kd, bvec=bvec, c_in=c_in,
            strict=(iw > tw) if reverse else (iw < tw), incl=(iw >= tw) if reverse else (iw <= tw),
            r_t=(r_ref[0].astype(F32) * jnp.exp(c_in)).astype(BF16),
            a_t=(-kk * jnp.exp(c_in - lw)).astype(BF16),
            k_t=(kd * e_neg).astype(BF16),
            b_t=(bvec * e_neg).astype(BF16)))

    units = [(di, ci) for ci in range(nch) for di in range(len(dirs))]
    n = range(len(units))
    dof = [dirs[di] for di, _ in units]
    sls = [slice(ci * CHUNK, (ci + 1) * CHUNK) for _, ci in units]
    c_tot = []
    for u in n:
        ci = units[u][1]
        last = ci * CHUNK if dof[u]["reverse"] else (ci + 1) * CHUNK - 1
        c_tot.append(dof[u]["c_in"][last:last + 1, :])
    e_hat = [jnp.exp(c_tot[u] - dof[u]["c_in"][sls[u]]) for u in n]
    kt_w = [_block_transpose(dof[u]["kd"][sls[u]] * e_hat[u]).astype(BF16) for u in n]
    bt_w = [_block_transpose(dof[u]["bvec"][sls[u]] * e_hat[u]).astype(BF16) for u in n]
    vb = [dof[u]["v_ref"][0, sls[u], :] for u in n]
    a_t = [dof[u]["a_t"][sls[u]] for u in n]
    r_t = [dof[u]["r_t"][sls[u]] for u in n]
    strict = [dof[u]["strict"] for u in n]
    incl = [dof[u]["incl"] for u in n]

    ar = [jnp.concatenate([a_t[u], r_t[u]], axis=0) for u in n]
    bk = [jnp.concatenate([_bd(dof[u]["b_t"][sls[u]], mb), _bd(dof[u]["k_t"][sls[u]], mb)], axis=0) for u in n]
    sbk = [_dot_nt(ar[u], bk[u]) for u in n]
    lab = [jnp.where(strict[u], sbk[u][:CHUNK, :gl], 0.0) for u in n]
    lak = [jnp.where(strict[u], sbk[u][:CHUNK, gl:], 0.0).astype(BF16) for u in n]
    arb = [jnp.where(incl[u], sbk[u][CHUNK:, :gl], 0.0).astype(BF16) for u in n]
    ark = [jnp.where(incl[u], sbk[u][CHUNK:, gl:], 0.0).astype(BF16) for u in n]
    la = [_dot(jnp.concatenate([lak[u], ark[u], kt_w[u]], axis=0), _bd(vb[u], mb)) for u in n]

    lb = [l.astype(BF16) for l in lab]
    p = [_dot(l, _bd(l, mb)) for l in lb]
    x = [eye_f + l for l in lab]
    n_lvl = int(np.log2(CHUNK)) - 1
    for lvl in range(n_lvl):
        pb = [q.astype(BF16) for q in p]
        if lvl < n_lvl - 1:
            xp = [_dot(jnp.concatenate([x[u].astype(BF16), pb[u]], axis=0), _bd(pb[u], mb)) for u in n]
            x = [x[u] + xp[u][:CHUNK] for u in n]
            p = [q[CHUNK:] for q in xp]
        else:
            x = [x[u] + _dot(x[u].astype(BF16), _bd(pb[u], mb)) for u in n]
    tb = [q.astype(BF16) for q in x]

    au = [_dot(tb[u], jnp.concatenate([_bd(a_t[u], mb), _bd(la[u][:CHUNK].astype(BF16), mb)], axis=1))
          for u in n]
    au_b = [q.astype(BF16) for q in au]
    rq = [_dot(jnp.concatenate([arb[u], bt_w[u]], axis=0),
               jnp.concatenate([_bd(au_b[u][:, :gl], mb), _bd(au_b[u][:, gl:], mb)], axis=1)) for u in n]
    r_hat = [(r_t[u].astype(F32) + rq[u][:CHUNK, :gl]).astype(BF16) for u in n]
    y_hat = [rq[u][:CHUNK, gl:] + la[u][CHUNK:2 * CHUNK] for u in n]
    p_w = [(rq[u][CHUNK:, :gl] + eye_f * jnp.exp(c_tot[u])).astype(BF16) for u in n]
    q_w = [rq[u][CHUNK:, gl:] + la[u][2 * CHUNK:] for u in n]

    s_refs = (sf_ref, sb_ref)
    y_refs = (yf_ref, yb_ref)
    state = [ref[...] for ref in s_refs]
    for step in range(nch):
        for di, d in enumerate(dirs):
            ci = nch - 1 - step if d["reverse"] else step
            u = units.index((di, ci))
            ys = _dot(jnp.concatenate([p_w[u], r_hat[u]], axis=0), _bd(state[di].astype(BF16), mb))
            state[di] = ys[:CHUNK] + q_w[u]
            y_refs[di][0, sls[u], :] = ys[CHUNK:] + y_hat[u]
    for di in range(len(dirs)):
        s_refs[di][...] = state[di]


def _wkv(r, k, v, kk, as0, lw0, as1, lw1, ka):
    bsz, t, d = r.shape
    tt = WKV_TILE
    nt = t // tt
    gl = LANES_PER_GROUP
    fwd = pl.BlockSpec((1, tt, gl), lambda b, g, i: (b, i, g))
    bwd = pl.BlockSpec((1, tt, gl), lambda b, g, i: (b, nt - 1 - i, g))
    ka_spec = pl.BlockSpec((1, gl), lambda b, g, i: (0, g))
    yshape = jax.ShapeDtypeStruct((bsz, t, d), F32)
    return pl.pallas_call(
        _wkv_kernel,
        grid=(bsz, d // gl, nt),
        in_specs=[fwd] * 6 + [bwd] * 6 + [ka_spec],
        out_specs=[fwd, bwd],
        out_shape=[yshape, yshape],
        scratch_shapes=[pltpu.VMEM((HEAD_DIM, gl), F32), pltpu.VMEM((HEAD_DIM, gl), F32)],
        compiler_params=pltpu.CompilerParams(
            dimension_semantics=("parallel", "parallel", "arbitrary"), vmem_limit_bytes=VMEM_LIMIT),
        name="wkv",
    )(r, k, v, kk, as0, lw0, r, k, v, kk, as1, lw1, ka)


def _rwkv_post_kernel(yf_ref, yb_ref, r_ref, k_ref, v_ref, as0_ref, as1_ref, z_ref, x_ref,
                      ka_ref, rk_ref, lnw_ref, lnb_ref, wo_ref, pg_ref, ng_ref, win_ref, bin_ref,
                      ones_ref, x1_out, q_out, k_out, v_out, z_out):
    ones_b = ones_ref[...]
    y = yf_ref[0] + yb_ref[0]
    mean = _head_sum(y, ones_b) * (1.0 / HEAD_DIM)
    yc = y - mean
    var = _head_sum(yc * yc, ones_b) * (1.0 / HEAD_DIM)
    yn = yc * lax.rsqrt(var + LNX_EPS) * lnw_ref[...] + lnb_ref[...]
    r = r_ref[0].astype(F32)
    k = k_ref[0].astype(F32)
    v = v_ref[0].astype(F32)
    ka = ka_ref[...]
    kd_sum = k * (1.0 + (as0_ref[0].astype(F32) - 1.0) * ka) + k * (1.0 + (as1_ref[0].astype(F32) - 1.0) * ka)
    bonus = _head_sum(r * kd_sum * rk_ref[...], ones_b) * v
    out = (yn + bonus) * _silu(z_ref[0].astype(F32))
    h = _dot(out.astype(BF16), wo_ref[...])
    x1 = x_ref[0] + _rms(h, pg_ref[...])
    x1_out[0] = x1
    xn = _rms(x1, ng_ref[...]).astype(BF16)
    e = D_MODEL
    for j, o_ref in enumerate((q_out, k_out, v_out, z_out)):
        pj = _dot(xn, win_ref[:, j * e:(j + 1) * e]) + bin_ref[:, j * e:(j + 1) * e]
        if j == 0:
            pj = pj * (HEAD_DIM ** -0.5)
        o_ref[0] = pj.astype(BF16)


def _rwkv_post(yf, yb, r, k, v, as0, as1, z, x, ka, rk, lnw, lnb, wo, pg, ng, win, b_in, ones_b):
    bsz, t, d = x.shape
    tt = TIME_TILE
    tile = pl.BlockSpec((1, tt, d), lambda b, i: (b, i, 0))
    consts = (ka, rk, lnw, lnb, wo, pg, ng, win, b_in, ones_b)
    bshape = jax.ShapeDtypeStruct((bsz, t, d), BF16)
    return pl.pallas_call(
        _rwkv_post_kernel,
        grid=(bsz, t // tt),
        in_specs=[tile] * 9 + [_const_spec(a.shape) for a in consts],
        out_specs=[tile] * 5,
        out_shape=[jax.ShapeDtypeStruct((bsz, t, d), F32)] + [bshape] * 4,
        compiler_params=pltpu.CompilerParams(
            dimension_semantics=("parallel", "parallel"), vmem_limit_bytes=VMEM_LIMIT),
        name="rwkv_post",
    )(yf, yb, r, k, v, as0, as1, z, x, *consts)


def _natten_kernel(q_ref, k_ref, v_ref, z_ref, x_ref, bias_ref, wo_ref, bo_ref, pg_ref,
                   out_ref, o_scr, *, rows):
    step = pl.program_id(1)
    gl = LANES_PER_GROUP
    kwin = WIN_ROWS * GRID_W
    r2 = lax.broadcasted_iota(jnp.int32, (gl, gl), 0)
    c2 = lax.broadcasted_iota(jnp.int32, (gl, gl), 1)
    bdmask = (r2 // HEAD_DIM) == (c2 // HEAD_DIM)

    for j in range(Q_ROWS):
        row = step * Q_ROWS + j
        rs = jnp.clip(row - WIN_ROWS // 2, 0, rows - WIN_ROWS)
        dr0 = rs - row + WIN_ROWS - 1
        k0 = pl.multiple_of(rs * GRID_W, GRID_W)
        qs = slice(j * GRID_W, (j + 1) * GRID_W)
        for g in range(N_GROUPS):
            gs = slice(g * gl, (g + 1) * gl)
            qb = q_ref[0, qs, gs]
            qst = jnp.where(bdmask, jnp.concatenate([qb] * HEADS_PER_GROUP, axis=0), jnp.zeros((), BF16))
            kw = k_ref[0, pl.ds(k0, kwin), gs]
            vw = v_ref[0, pl.ds(k0, kwin), gs]
            s = _dot_nt(qst, kw)
            bias = jnp.concatenate([bias_ref[g, dr0 + 2 * p] for p in range(WIN_ROWS // 2)], axis=-1)
            s = s + bias
            m = jnp.max(s, axis=-1, keepdims=True)
            p = jnp.exp(s - m)
            l = jnp.sum(p, axis=-1, keepdims=True)
            pv = _dot(p.astype(BF16), vw) * (1.0 / l)
            pv = jnp.where(bdmask, pv, 0.0)
            o = pv[0:GRID_W]
            for h in range(1, HEADS_PER_GROUP):
                o = o + pv[h * GRID_W:(h + 1) * GRID_W]
            o_scr[qs, gs] = o

    gated = o_scr[...] * _silu(z_ref[0].astype(F32))
    h = _dot(gated.astype(BF16), wo_ref[...]) + bo_ref[...]
    out_ref[0] = x_ref[0] + _rms(h, pg_ref[...])


def _natten(q, k, v, z, x1, bias_tab, wo, bo, pg):
    bsz, t, d = x1.shape
    rows = t // GRID_W
    tq = Q_ROWS * GRID_W
    tile = pl.BlockSpec((1, tq, d), lambda b, i: (b, i, 0))
    full = pl.BlockSpec((1, t, d), lambda b, i: (b, 0, 0))
    consts = (bias_tab, wo, bo, pg)
    return pl.pallas_call(
        functools.partial(_natten_kernel, rows=rows),
        grid=(bsz, rows // Q_ROWS),
        in_specs=[tile, full, full, tile, tile] + [_const_spec(a.shape) for a in consts],
        out_specs=tile,
        out_shape=jax.ShapeDtypeStruct((bsz, t, d), F32),
        scratch_shapes=[pltpu.VMEM((tq, d), F32)],
        compiler_params=pltpu.CompilerParams(
            dimension_semantics=("parallel", "arbitrary"), vmem_limit_bytes=VMEM_LIMIT),
        name="natten",
    )(q, k, v, z, x1, *consts)


def _bias_table(rpb):
    qc = np.arange(GRID_W)[:, None]
    kc = np.arange(GRID_W)[None, :]
    ws = np.clip(qc - WIN_COLS // 2, 0, GRID_W - WIN_COLS)
    valid = (kc >= ws) & (kc < ws + WIN_COLS)
    rel = np.clip(kc - qc + WIN_COLS - 1, 0, 2 * WIN_COLS - 2)
    tab = jnp.where(jnp.asarray(valid)[None, None], rpb[:, :, rel], NEG_INF)
    two = jnp.concatenate([tab[:, :-1], tab[:, 1:]], axis=-1)
    two = two.reshape(N_GROUPS, HEADS_PER_GROUP, 2 * WIN_ROWS - 2, GRID_W, 2 * GRID_W)
    return jnp.transpose(two, (0, 2, 1, 3, 4)).reshape(
        N_GROUPS, 2 * WIN_ROWS - 2, HEADS_PER_GROUP * GRID_W, 2 * GRID_W).astype(F32)


def _trunk(x, p):
    row = lambda a: a.reshape(1, -1)
    r, k, v, z, kk, as0, as1, lw0, lw1 = _rwkv_pre(
        x, row(p["pre_g"][0]), p["mu"], p["w_r"], p["w_k"], p["w_v"], p["w_z"], p["w0"], p["w1"], p["w2"],
        p["a0"], p["a1"], p["a2"], row(p["k_k"]), p["ones"])
    ka = row(p["k_a"])
    yf, yb = _wkv(r, k, v, kk, as0, lw0, as1, lw1, ka)
    x1, q, k2, v2, z2 = _rwkv_post(
        yf, yb, r, k, v, as0, as1, z, x, ka, row(p["r_k"]), row(p["lnx_w"]), row(p["lnx_b"]), p["rk_w_o"],
        row(p["post_g"][0]), row(p["pre_g"][1]), p["w_in"], row(p["b_in"]), p["ones"])
    return _natten(q, k2, v2, z2, x1, p["bias_tab"], p["na_w_o"], row(p["b_o"]), row(p["post_g"][1]))


def kernel(x_prompt, x_sample, pre_norm_g, post_norm_g, rk_mu, rk_w_r, rk_w_k, rk_w_v, rk_w_z, rk_w0, rk_w1, rk_w2, rk_a0, rk_a1, rk_a2, rk_k_k, rk_k_a, rk_r_k, rk_lnx_w, rk_lnx_b, rk_w_o, na_w_in, na_b_in, na_rpb, na_w_o, na_b_o):
    bf = lambda a: a.astype(BF16)
    hd = np.arange(LANES_PER_GROUP) // HEAD_DIM
    p = dict(
        pre_g=pre_norm_g, post_g=post_norm_g, mu=rk_mu[0],
        w_r=bf(rk_w_r[0]), w_k=bf(rk_w_k[0]), w_v=bf(rk_w_v[0]), w_z=bf(rk_w_z[0]),
        w0=rk_w0[0], w1=bf(rk_w1[0]), w2=bf(rk_w2[0]), a0=rk_a0[0], a1=bf(rk_a1[0]), a2=bf(rk_a2[0]),
        k_k=rk_k_k[0], k_a=rk_k_a[0], r_k=rk_r_k[0], lnx_w=rk_lnx_w[0], lnx_b=rk_lnx_b[0],
        rk_w_o=bf(rk_w_o[0]), w_in=bf(na_w_in[0]), b_in=na_b_in[0], na_w_o=bf(na_w_o[0]), b_o=na_b_o[0],
        bias_tab=_bias_table(na_rpb[0]),
        ones=jnp.asarray(hd[:, None] == hd[None, :], dtype=BF16),
    )
    return (_trunk(x_prompt, p), _trunk(x_sample, p))
```

```python
import functools

import jax
import jax.numpy as jnp
import numpy as np
from jax import lax
from jax.experimental import pallas as pl
from jax.experimental.pallas import tpu as pltpu

D_MODEL = 1024
N_HEADS = 16
HEAD_DIM = 64
LORA = 64
RMS_EPS = 1e-6
LNX_EPS = 64e-5
GRID_W = 64
WIN_ROWS = 8
WIN_COLS = 16
NEG_INF = -1e30

LANE_TILE = 128
HEADS_PER_TILE = LANE_TILE // HEAD_DIM
LANES_PER_GROUP = 256
HEADS_PER_GROUP = LANES_PER_GROUP // HEAD_DIM
N_GROUPS = D_MODEL // LANES_PER_GROUP
CHUNK = 64
TIME_TILE = 512
ROW_SUB = 512
WKV_TILE = 512
WKV_SUB = 128
Q_ROWS = 4
KEY_BLOCK = WIN_COLS
Q_HALF = WIN_COLS // 2


def _query_ranges():
    ranges = []
    for q0 in range(0, GRID_W, Q_HALF):
        lo = min(max(q0 - WIN_COLS // 2, 0), GRID_W - WIN_COLS)
        fb = min(lo // KEY_BLOCK, GRID_W // KEY_BLOCK - 2)
        hi = min(max(q0 + Q_HALF - 1 - WIN_COLS // 2, 0), GRID_W - WIN_COLS) + WIN_COLS - 1
        assert fb * KEY_BLOCK <= lo and hi < (fb + 2) * KEY_BLOCK
        if ranges and ranges[-1][2] == fb:
            ranges[-1] = (ranges[-1][0], q0 + Q_HALF, fb)
        else:
            ranges.append((q0, q0 + Q_HALF, fb))
    return tuple(ranges)


_Q_RANGES = _query_ranges()
VMEM_LIMIT = 56 * 1024 * 1024

F32 = jnp.float32
BF16 = jnp.bfloat16


def _dot(a, b):
    return jnp.dot(a, b, preferred_element_type=F32)


def _dot_nt(a, b):
    return lax.dot_general(a, b, (((1,), (1,)), ((), ())), preferred_element_type=F32)


def _dot_tn(a, b):
    return lax.dot_general(a, b, (((0,), (0,)), ((), ())), preferred_element_type=F32)


def _split_dot(x, ones_b):
    hi = x.astype(BF16)
    lo = (x - hi.astype(F32)).astype(BF16)
    return _dot(hi, ones_b) + _dot(lo, ones_b)


def _split_dot_left(ones_b, x):
    hi = x.astype(BF16)
    lo = (x - hi.astype(F32)).astype(BF16)
    return _dot(ones_b, hi) + _dot(ones_b, lo)


def _head_sum(x, ones_b, split):
    outs = []
    for g in range(x.shape[-1] // LANES_PER_GROUP):
        xs = x[:, g * LANES_PER_GROUP:(g + 1) * LANES_PER_GROUP]
        outs.append(_split_dot(xs, ones_b) if split else _dot(xs.astype(BF16), ones_b))
    return jnp.concatenate(outs, axis=-1) if len(outs) > 1 else outs[0]


def _rms(x, g):
    return x * lax.rsqrt(jnp.mean(x * x, axis=-1, keepdims=True) + RMS_EPS) * g


def _silu(z):
    return z * (1.0 / (1.0 + jnp.exp(-z)))


def _const_spec(shape):
    nd = len(shape)
    return pl.BlockSpec(shape, lambda *_: (0,) * nd, pipeline_mode=pl.Buffered(1))


def _rwkv_pre_kernel(x_ref, xp_ref, xq_ref, g_ref, mu_ref, wr_ref, wk_ref, wv_ref, wz_ref,
                     w0_ref, wd_ref, w2_ref, a0_ref, a2_ref, kkp_ref, ones_ref,
                     r_out, k_out, v_out, z_out, kk_out, as0_out, as1_out, lw0_out, lw1_out):
    i = pl.program_id(1)
    nt = pl.num_programs(1)
    tt = x_ref.shape[1]
    g = g_ref[...]
    n_sub = tt // ROW_SUB
    row = lax.broadcasted_iota(jnp.int32, (ROW_SUB, 1), 0)
    prev_row = jnp.where(i == 0, 0.0, _rms(xp_ref[0], g)[7:8])
    for s in range(n_sub):
        rows = slice(s * ROW_SUB, (s + 1) * ROW_SUB)
        xn = _rms(x_ref[0, rows, :], g)
        if s == n_sub - 1:
            next_row = jnp.where(i == nt - 1, 0.0, _rms(xq_ref[0], g)[0:1])
        else:
            next_row = _rms(x_ref[0, (s + 1) * ROW_SUB:(s + 1) * ROW_SUB + 8, :], g)[0:1]
        x_prev = jnp.where(row == 0, prev_row, pltpu.roll(xn, 1, axis=0))
        x_next = jnp.where(row == ROW_SUB - 1, next_row, pltpu.roll(xn, ROW_SUB - 1, axis=0))
        xx = 0.5 * (x_prev + x_next) - xn
        prev_row = xn[ROW_SUB - 1:ROW_SUB]

        def lerp(j):
            return (xn + xx * mu_ref[j:j + 1, :]).astype(BF16)

        dm = x_ref.shape[2]
        xn_b = xn.astype(BF16)
        down = _dot(xn_b, wd_ref[:dm, :]) + _dot(xx.astype(BF16), wd_ref[dm:, :])
        hw = jnp.tanh(down[:, :2 * LORA]).astype(BF16)
        ha = down[:, 2 * LORA:].astype(BF16)

        def sigmoid(u):
            return 0.5 + 0.5 * jnp.tanh(0.5 * u)

        def log_decay(d):
            wl = w0_ref[d:d + 1, :] + _dot(hw, w2_ref[d])
            return (-float(np.exp(-0.5))) * sigmoid(wl)

        def iclr(d):
            return sigmoid(a0_ref[d:d + 1, :] + _dot(ha, a2_ref[d])).astype(BF16)

        r_out[0, rows, :] = _dot(lerp(0), wr_ref[...]).astype(BF16)
        lw0_out[0, rows, :] = log_decay(0)
        k = _dot(lerp(1), wk_ref[...])
        k_out[0, rows, :] = k.astype(BF16)
        as0_out[0, rows, :] = iclr(0)
        v_out[0, rows, :] = _dot(lerp(2), wv_ref[...]).astype(BF16)
        lw1_out[0, rows, :] = log_decay(1)
        z_out[0, rows, :] = _dot(xn_b, wz_ref[...]).astype(BF16)
        as1_out[0, rows, :] = iclr(1)

        kk = k * kkp_ref[...]
        ss = _head_sum(kk * kk, ones_ref[...], False)
        kk_out[0, rows, :] = (kk * lax.rsqrt(jnp.maximum(ss, 1e-24))).astype(BF16)


def _rwkv_pre(x, g, mu, wr, wk, wv, wz, w0, wd, w2, a0, a2, kkp, ones_b):
    bsz, t, d = x.shape
    tt = TIME_TILE
    nt = t // tt
    rb = tt // 8
    tile = pl.BlockSpec((1, tt, d), lambda b, i: (b, i, 0))
    prev = pl.BlockSpec((1, 8, d), lambda b, i: (b, jnp.maximum(i * rb - 1, 0), 0))
    nxt = pl.BlockSpec((1, 8, d), lambda b, i: (b, jnp.minimum((i + 1) * rb, t // 8 - 1), 0))
    ins = [tile, prev, nxt] + [_const_spec(a.shape) for a in
                               (g, mu, wr, wk, wv, wz, w0, wd, w2, a0, a2, kkp, ones_b)]
    bshape = jax.ShapeDtypeStruct((bsz, t, d), BF16)
    fshape = jax.ShapeDtypeStruct((bsz, t, d), F32)
    return pl.pallas_call(
        _rwkv_pre_kernel,
        grid=(bsz, nt),
        in_specs=ins,
        out_specs=[tile] * 9,
        out_shape=[bshape] * 7 + [fshape] * 2,
        compiler_params=pltpu.CompilerParams(
            dimension_semantics=("parallel", "parallel"), vmem_limit_bytes=VMEM_LIMIT),
        name="rwkv_pre",
    )(x, x, x, g, mu, wr, wk, wv, wz, w0, wd, w2, a0, a2, kkp, ones_b)


def _head_lane_masks(rows, dtype):
    lane = lax.broadcasted_iota(jnp.int32, (rows, LANE_TILE), 1)
    return [jnp.where((lane // HEAD_DIM) == j, 1.0, 0.0).astype(dtype) for j in range(HEADS_PER_TILE)]


def _bd(x, masks):
    zero = jnp.zeros((x.shape[0], LANE_TILE), x.dtype)
    n_tiles = x.shape[1] // LANE_TILE
    blocks = []
    for h in range(HEADS_PER_GROUP):
        lt, j = divmod(h, HEADS_PER_TILE)
        piece = x[:, lt * LANE_TILE:(lt + 1) * LANE_TILE] * masks[j]
        blocks.append(jnp.concatenate([piece if t == lt else zero for t in range(n_tiles)], axis=1))
    return jnp.concatenate(blocks, axis=0)


def _block_transpose(x):
    tiles = []
    for lt in range(x.shape[1] // LANE_TILE):
        t = x[:, lt * LANE_TILE:(lt + 1) * LANE_TILE].T
        tiles.append(jnp.concatenate([t[j * HEAD_DIM:(j + 1) * HEAD_DIM] for j in range(HEADS_PER_TILE)], axis=1))
    return jnp.concatenate(tiles, axis=1)


def _wkv_kernel(rf_ref, kf_ref, vf_ref, kkf_ref, asf_ref, lwf_ref,
                rb_ref, kb_ref, vb_ref, kkb_ref, asb_ref, lwb_ref, ka_ref,
                yf_ref, yb_ref, sf_ref, sb_ref, pw_scr, rh_scr, qw_scr, yh_scr, *, nt):
    tt = rf_ref.shape[1]
    nch = tt // CHUNK
    gl = LANES_PER_GROUP
    step_id = pl.program_id(0)
    slot = step_id % 2
    prev = 1 - slot

    @pl.when(step_id == 0)
    def _():
        sf_ref[...] = jnp.zeros_like(sf_ref)
        sb_ref[...] = jnp.zeros_like(sb_ref)
        for scr in (pw_scr, rh_scr, qw_scr, yh_scr):
            scr[...] = jnp.zeros_like(scr)

    mb = _head_lane_masks(CHUNK, BF16)
    unit_of = {(di, ci): ci * 2 + di for ci in range(nch) for di in range(2)}

    first_tile = (jnp.maximum(step_id - 1, 0) % nt) == 0
    s_refs = (sf_ref, sb_ref)
    y_refs = (yf_ref, yb_ref)
    state = [jnp.where(first_tile, 0.0, ref[...]) for ref in s_refs]

    def seq_step(step):
        for di in range(2):
            ci = nch - 1 - step if di == 1 else step
            u = unit_of[(di, ci)]
            ys = _dot(jnp.concatenate([pw_scr[prev, u], rh_scr[prev, u]], axis=0),
                      _bd(state[di].astype(BF16), mb))
            state[di] = ys[:CHUNK] + qw_scr[prev, u]
            y_refs[di][0, ci * CHUNK:(ci + 1) * CHUNK, :] = ys[CHUNK:] + yh_scr[prev, u]

    tw = lax.broadcasted_iota(jnp.int32, (CHUNK, gl), 0)
    iw = lax.broadcasted_iota(jnp.int32, (CHUNK, gl), 1) % CHUNK
    eye_f = jnp.where(iw == tw, 1.0, 0.0)
    rt = lax.broadcasted_iota(jnp.int32, (WKV_SUB, WKV_SUB), 0)
    ct = lax.broadcasted_iota(jnp.int32, (WKV_SUB, WKV_SUB), 1)
    same = (rt // CHUNK) == (ct // CHUNK)
    tri = [jnp.where(same & (ct <= rt), 1.0, 0.0).astype(BF16), jnp.where(same & (ct >= rt), 1.0, 0.0).astype(BF16)]
    strict_of = [iw < tw, iw > tw]
    incl_of = [iw <= tw, iw >= tw]
    ka = ka_ref[...]
    in_refs = ((rf_ref, kf_ref, vf_ref, kkf_ref, asf_ref, lwf_ref), (rb_ref, kb_ref, vb_ref, kkb_ref, asb_ref, lwb_ref))

    units, c_tot, kt_w, bt_w, vb, a_t, r_t, strict, incl = [], [], [], [], [], [], [], [], []
    lab, lak, arb, ark, la, p = [], [], [], [], [], []
    n_sub = tt // WKV_SUB
    for w in range(n_sub):
        first = len(units)
        for di in range(2):
            r_ref, k_ref, v_ref, kk_ref, as_ref, lw_ref = in_refs[di]
            rows = slice(w * WKV_SUB, (w + 1) * WKV_SUB)
            kk = kk_ref[0, rows, :].astype(F32)
            asg = as_ref[0, rows, :].astype(F32)
            lw = lw_ref[0, rows, :]
            kd = k_ref[0, rows, :].astype(F32) * (1.0 + (asg - 1.0) * ka)
            bvec = kk * asg
            c_in = _split_dot_left(tri[di], lw)
            e_neg = jnp.exp(-c_in)
            rt_s = (r_ref[0, rows, :].astype(F32) * jnp.exp(c_in)).astype(BF16)
            at_s = (-kk * jnp.exp(c_in - lw)).astype(BF16)
            kt_s = (kd * e_neg).astype(BF16)
            bt_s = (bvec * e_neg).astype(BF16)
            for cs in range(WKV_SUB // CHUNK):
                sl = slice(cs * CHUNK, (cs + 1) * CHUNK)
                ci = w * (WKV_SUB // CHUNK) + cs
                last = cs * CHUNK if di == 1 else (cs + 1) * CHUNK - 1
                ct_u = c_in[last:last + 1, :]
                e_hat = jnp.exp(ct_u - c_in[sl])
                units.append((di, ci))
                c_tot.append(ct_u)
                kt_w.append(_block_transpose(kd[sl] * e_hat).astype(BF16))
                bt_w.append(_block_transpose(bvec[sl] * e_hat).astype(BF16))
                vb.append(v_ref[0, ci * CHUNK:(ci + 1) * CHUNK, :])
                a_t.append(at_s[sl])
                r_t.append(rt_s[sl])
                strict.append(strict_of[di])
                incl.append(incl_of[di])
                ar = jnp.concatenate([at_s[sl], rt_s[sl]], axis=0)
                bk = jnp.concatenate([_bd(bt_s[sl], mb), _bd(kt_s[sl], mb)], axis=0)
                lab.append(None)
                lak.append((ar, bk))
            seq_step(2 * w + di)
        for u in range(first, len(units)):
            ar, bk = lak[u]
            sbk = _dot_nt(ar, bk)
            lab[u] = jnp.where(strict[u], sbk[:CHUNK, :gl], 0.0)
            lak[u] = jnp.where(strict[u], sbk[:CHUNK, gl:], 0.0).astype(BF16)
            arb.append(jnp.where(incl[u], sbk[CHUNK:, :gl], 0.0).astype(BF16))
            ark.append(jnp.where(incl[u], sbk[CHUNK:, gl:], 0.0).astype(BF16))
        for u in range(first, len(units)):
            la.append(_dot(jnp.concatenate([lak[u], ark[u], kt_w[u]], axis=0), _bd(vb[u], mb)))
            lb = lab[u].astype(BF16)
            p.append(_dot(lb, _bd(lb, mb)))
    for di in range(2):
        s_refs[di][...] = state[di]
    n = range(len(units))

    x = [eye_f + l for l in lab]
    n_lvl = int(np.log2(CHUNK)) - 1
    for lvl in range(n_lvl):
        pb = [q.astype(BF16) for q in p]
        if lvl < n_lvl - 1:
            xp = [_dot(jnp.concatenate([x[u].astype(BF16), pb[u]], axis=0), _bd(pb[u], mb)) for u in n]
            x = [x[u] + xp[u][:CHUNK] for u in n]
            p = [q[CHUNK:] for q in xp]
        else:
            x = [x[u] + _dot(x[u].astype(BF16), _bd(pb[u], mb)) for u in n]
    tb = [q.astype(BF16) for q in x]

    au = [_dot(tb[u], jnp.concatenate([_bd(a_t[u], mb), _bd(la[u][:CHUNK].astype(BF16), mb)], axis=1))
          for u in n]
    au_b = [q.astype(BF16) for q in au]
    rq = [_dot(jnp.concatenate([arb[u], bt_w[u]], axis=0),
               jnp.concatenate([_bd(au_b[u][:, :gl], mb), _bd(au_b[u][:, gl:], mb)], axis=1)) for u in n]
    for u in n:
        key = unit_of[units[u]]
        rh_scr[slot, key] = (r_t[u].astype(F32) + rq[u][:CHUNK, :gl]).astype(BF16)
        yh_scr[slot, key] = rq[u][:CHUNK, gl:] + la[u][CHUNK:2 * CHUNK]
        pw_scr[slot, key] = (rq[u][CHUNK:, :gl] + eye_f * jnp.exp(c_tot[u])).astype(BF16)
        qw_scr[slot, key] = rq[u][CHUNK:, gl:] + la[u][2 * CHUNK:]


def _wkv(r, k, v, kk, as0, lw0, as1, lw1, ka):
    bsz, t, d = r.shape
    tt = WKV_TILE
    nt = t // tt
    gl = LANES_PER_GROUP
    ng = d // gl
    n_items = bsz * ng * nt

    def item(m):
        return m // (ng * nt), (m // nt) % ng, m % nt

    def in_map(reverse):
        def index(s):
            b, g, i = item(jnp.minimum(s, n_items - 1))
            return b, (nt - 1 - i) if reverse else i, g
        return index

    def out_map(reverse):
        def index(s):
            b, g, i = item(jnp.maximum(s - 1, 0))
            return b, (nt - 1 - i) if reverse else i, g
        return index

    tile = (1, tt, gl)
    fwd, bwd = pl.BlockSpec(tile, in_map(False)), pl.BlockSpec(tile, in_map(True))
    ka_spec = pl.BlockSpec((1, gl), lambda s: (0, item(jnp.minimum(s, n_items - 1))[1]))
    yshape = jax.ShapeDtypeStruct((bsz, t, d), F32)
    n_units = 2 * (tt // CHUNK)
    return pl.pallas_call(
        functools.partial(_wkv_kernel, nt=nt),
        grid=(n_items + 1,),
        in_specs=[fwd] * 6 + [bwd] * 6 + [ka_spec],
        out_specs=[pl.BlockSpec(tile, out_map(False)), pl.BlockSpec(tile, out_map(True))],
        out_shape=[yshape, yshape],
        scratch_shapes=[pltpu.VMEM((HEAD_DIM, gl), F32), pltpu.VMEM((HEAD_DIM, gl), F32),
                        pltpu.VMEM((2, n_units, CHUNK, gl), BF16), pltpu.VMEM((2, n_units, CHUNK, gl), BF16),
                        pltpu.VMEM((2, n_units, CHUNK, gl), F32), pltpu.VMEM((2, n_units, CHUNK, gl), F32)],
        compiler_params=pltpu.CompilerParams(
            dimension_semantics=("arbitrary",), vmem_limit_bytes=VMEM_LIMIT),
        name="wkv",
    )(r, k, v, kk, as0, lw0, r, k, v, kk, as1, lw1, ka)


def _rwkv_post_kernel(yf_ref, yb_ref, r_ref, k_ref, v_ref, as0_ref, as1_ref, z_ref, x_ref,
                      ka_ref, rk_ref, lnw_ref, lnb_ref, wo_ref, pg_ref, ng_ref, win_ref, bin_ref,
                      ones_ref, x1_out, q_out, k_out, v_out, z_out):
    ones_b = ones_ref[...]
    ka = ka_ref[...]
    e = D_MODEL
    for s in range(yf_ref.shape[1] // ROW_SUB):
        rows = slice(s * ROW_SUB, (s + 1) * ROW_SUB)
        y = yf_ref[0, rows, :] + yb_ref[0, rows, :]
        mean = _head_sum(y, ones_b, True) * (1.0 / HEAD_DIM)
        yc = y - mean
        var = _head_sum(yc * yc, ones_b, False) * (1.0 / HEAD_DIM)
        yn = yc * lax.rsqrt(var + LNX_EPS) * lnw_ref[...] + lnb_ref[...]
        r = r_ref[0, rows, :].astype(F32)
        k = k_ref[0, rows, :].astype(F32)
        v = v_ref[0, rows, :].astype(F32)
        a_sum = as0_ref[0, rows, :].astype(F32) + as1_ref[0, rows, :].astype(F32)
        kd_sum = k * (2.0 + (a_sum - 2.0) * ka)
        bonus = _head_sum(r * kd_sum * rk_ref[...], ones_b, False) * v
        out = (yn + bonus) * _silu(z_ref[0, rows, :].astype(F32))
        h = _dot(out.astype(BF16), wo_ref[...])
        x1 = x_ref[0, rows, :] + _rms(h, pg_ref[...])
        x1_out[0, rows, :] = x1
        xn = _rms(x1, ng_ref[...]).astype(BF16)
        for j, o_ref in enumerate((q_out, k_out, v_out, z_out)):
            pj = _dot(xn, win_ref[:, j * e:(j + 1) * e]) + bin_ref[:, j * e:(j + 1) * e]
            if j == 0:
                pj = pj * (HEAD_DIM ** -0.5)
            o_ref[0, rows, :] = pj.astype(BF16)


def _rwkv_post(yf, yb, r, k, v, as0, as1, z, x, ka, rk, lnw, lnb, wo, pg, ng, win, b_in, ones_b):
    bsz, t, d = x.shape
    tt = TIME_TILE
    tile = pl.BlockSpec((1, tt, d), lambda b, i: (b, i, 0))
    consts = (ka, rk, lnw, lnb, wo, pg, ng, win, b_in, ones_b)
    bshape = jax.ShapeDtypeStruct((bsz, t, d), BF16)
    return pl.pallas_call(
        _rwkv_post_kernel,
        grid=(bsz, t // tt),
        in_specs=[tile] * 9 + [_const_spec(a.shape) for a in consts],
        out_specs=[tile] * 5,
        out_shape=[jax.ShapeDtypeStruct((bsz, t, d), F32)] + [bshape] * 4,
        compiler_params=pltpu.CompilerParams(
            dimension_semantics=("parallel", "parallel"), vmem_limit_bytes=VMEM_LIMIT),
        name="rwkv_post",
    )(yf, yb, r, k, v, as0, as1, z, x, *consts)


def _natten_kernel(q_ref, k_ref, v_ref, z_ref, x_ref, bias_ref, wo_ref, bo_ref, pg_ref,
                   out_ref, o_scr, *, rows):
    step = pl.program_id(1)
    gl = LANES_PER_GROUP
    n_cb = GRID_W // KEY_BLOCK
    lane = lax.broadcasted_iota(jnp.int32, (GRID_W, gl), 1) // HEAD_DIM
    hmask_f = [jnp.where(lane == h, 1.0, 0.0) for h in range(HEADS_PER_GROUP)]
    hmask_b = [m.astype(BF16) for m in hmask_f]

    for j in range(Q_ROWS):
        row = step * Q_ROWS + j
        rs = jnp.clip(row - WIN_ROWS // 2, 0, rows - WIN_ROWS)
        dr0 = rs - row + WIN_ROWS - 1
        k0 = rs * GRID_W
        qs = slice(j * GRID_W, (j + 1) * GRID_W)
        starts = [pl.multiple_of(k0 + s * GRID_W + c * KEY_BLOCK, KEY_BLOCK)
                  for c in range(n_cb) for s in range(WIN_ROWS)]
        for g in range(N_GROUPS):
            gs = slice(g * gl, (g + 1) * gl)
            kw = jnp.concatenate([k_ref[0, pl.ds(st, KEY_BLOCK), gs] for st in starts], axis=0)
            vw = jnp.concatenate([v_ref[0, pl.ds(st, KEY_BLOCK), gs] for st in starts], axis=0)
            qb = q_ref[0, qs, gs]
            qst = jnp.concatenate([qb * hmask_b[h] for h in range(HEADS_PER_GROUP)], axis=0)
            s = _dot_nt(qst, kw)
            p_rows, inv_l = [], []
            for h in range(HEADS_PER_GROUP):
                for q0, q1, fb in _Q_RANGES:
                    rsl = slice(h * GRID_W + q0, h * GRID_W + q1)
                    sh = s[rsl, fb * LANE_TILE:(fb + 2) * LANE_TILE] + bias_ref[g, dr0, rsl, :]
                    m = jnp.max(sh, axis=-1, keepdims=True)
                    p = jnp.exp(sh - m)
                    inv_l.append(1.0 / jnp.sum(p, axis=-1, keepdims=True))
                    zl = [jnp.zeros((q1 - q0, LANE_TILE), F32)] * fb
                    zr = [jnp.zeros((q1 - q0, LANE_TILE), F32)] * (n_cb - 2 - fb)
                    p_rows.append(jnp.concatenate(zl + [p] + zr, axis=1))
            pv = _dot(jnp.concatenate(p_rows, axis=0).astype(BF16), vw)
            pv = pv * jnp.concatenate(inv_l, axis=0)
            o = pv[0:GRID_W] * hmask_f[0]
            for h in range(1, HEADS_PER_GROUP):
                o = o + pv[h * GRID_W:(h + 1) * GRID_W] * hmask_f[h]
            o_scr[qs, gs] = o

    gated = o_scr[...] * _silu(z_ref[0].astype(F32))
    h = _dot(gated.astype(BF16), wo_ref[...]) + bo_ref[...]
    out_ref[0] = x_ref[0] + _rms(h, pg_ref[...])


def _natten(q, k, v, z, x1, bias_tab, wo, bo, pg):
    bsz, t, d = x1.shape
    rows = t // GRID_W
    tq = Q_ROWS * GRID_W
    tile = pl.BlockSpec((1, tq, d), lambda b, i: (b, i, 0))
    full = pl.BlockSpec((1, t, d), lambda b, i: (b, 0, 0))
    consts = (bias_tab, wo, bo, pg)
    return pl.pallas_call(
        functools.partial(_natten_kernel, rows=rows),
        grid=(bsz, rows // Q_ROWS),
        in_specs=[tile, full, full, tile, tile] + [_const_spec(a.shape) for a in consts],
        out_specs=tile,
        out_shape=jax.ShapeDtypeStruct((bsz, t, d), F32),
        scratch_shapes=[pltpu.VMEM((tq, d), F32)],
        compiler_params=pltpu.CompilerParams(
            dimension_semantics=("parallel", "arbitrary"), vmem_limit_bytes=VMEM_LIMIT),
        name="natten",
    )(q, k, v, z, x1, *consts)


def _bias_table(rpb):
    shape = (WIN_ROWS, HEADS_PER_GROUP, GRID_W, 2, WIN_ROWS, KEY_BLOCK)
    dr0, h, q, t, s, kcl = np.indices(shape)
    first_block = np.zeros(GRID_W, np.int64)
    for q0, q1, fb in _Q_RANGES:
        first_block[q0:q1] = fb
    kc = (first_block[q] + t) * KEY_BLOCK + kcl
    ws = np.clip(q - WIN_COLS // 2, 0, GRID_W - WIN_COLS)
    valid = (kc >= ws) & (kc < ws + WIN_COLS)
    dr = np.minimum(dr0 + s, 2 * WIN_ROWS - 2)
    rel = np.clip(kc - q + WIN_COLS - 1, 0, 2 * WIN_COLS - 2)
    tabs = []
    for g in range(N_GROUPS):
        vals = rpb[g * HEADS_PER_GROUP + h, dr, rel]
        tabs.append(jnp.where(jnp.asarray(valid), vals, NEG_INF).reshape(
            WIN_ROWS, HEADS_PER_GROUP * GRID_W, 2 * WIN_ROWS * KEY_BLOCK))
    return jnp.stack(tabs).astype(F32)


def _trunk(x, p):
    row = lambda a: a.reshape(1, -1)
    r, k, v, z, kk, as0, as1, lw0, lw1 = _rwkv_pre(
        x, row(p["pre_g"][0]), p["mu"], p["w_r"], p["w_k"], p["w_v"], p["w_z"], p["w0"], p["w_down"], p["w2"],
        p["a0"], p["a2"], row(p["k_k"]), p["ones"])
    ka = row(p["k_a"])
    yf, yb = _wkv(r, k, v, kk, as0, lw0, as1, lw1, ka)
    x1, q, k2, v2, z2 = _rwkv_post(
        yf, yb, r, k, v, as0, as1, z, x, ka, row(p["r_k"]), row(p["lnx_w"]), row(p["lnx_b"]), p["rk_w_o"],
        row(p["post_g"][0]), row(p["pre_g"][1]), p["w_in"], row(p["b_in"]), p["ones"])
    return _natten(q, k2, v2, z2, x1, p["bias_tab"], p["na_w_o"], row(p["b_o"]), row(p["post_g"][1]))


def kernel(x_prompt, x_sample, pre_norm_g, post_norm_g, rk_mu, rk_w_r, rk_w_k, rk_w_v, rk_w_z, rk_w0, rk_w1, rk_w2, rk_a0, rk_a1, rk_a2, rk_k_k, rk_k_a, rk_r_k, rk_lnx_w, rk_lnx_b, rk_w_o, na_w_in, na_b_in, na_rpb, na_w_o, na_b_o):
    bf = lambda a: a.astype(BF16)
    hd = np.arange(LANES_PER_GROUP) // HEAD_DIM
    downs = (rk_w1[0, 0], rk_w1[0, 1], rk_a1[0, 0], rk_a1[0, 1])
    lora_down = jnp.concatenate(downs, axis=1)
    lora_down_mu = jnp.concatenate([rk_mu[0, 3 + j][:, None] * w for j, w in enumerate(downs)], axis=1)
    zpad = jnp.zeros((LORA, D_MODEL), F32)
    p = dict(
        pre_g=pre_norm_g, post_g=post_norm_g, mu=rk_mu[0],
        w_r=bf(rk_w_r[0]), w_k=bf(rk_w_k[0]), w_v=bf(rk_w_v[0]), w_z=bf(rk_w_z[0]),
        w0=rk_w0[0], a0=rk_a0[0], w_down=bf(jnp.concatenate([lora_down, lora_down_mu], axis=0)),
        w2=bf(jnp.stack([jnp.concatenate([rk_w2[0, 0], zpad], axis=0), jnp.concatenate([zpad, rk_w2[0, 1]], axis=0)])),
        a2=bf(jnp.stack([jnp.concatenate([rk_a2[0, 0], zpad], axis=0), jnp.concatenate([zpad, rk_a2[0, 1]], axis=0)])),
        k_k=rk_k_k[0], k_a=rk_k_a[0], r_k=rk_r_k[0], lnx_w=rk_lnx_w[0], lnx_b=rk_lnx_b[0],
        rk_w_o=bf(rk_w_o[0]), w_in=bf(na_w_in[0]), b_in=na_b_in[0], na_w_o=bf(na_w_o[0]), b_o=na_b_o[0],
        bias_tab=_bias_table(na_rpb[0]),
        ones=jnp.asarray(hd[:, None] == hd[None, :], dtype=BF16),
    )
    return (_trunk(x_prompt, p), _trunk(x_sample, p))
```

```python
import functools

import jax
import jax.numpy as jnp
import numpy as np
from jax import lax
from jax.experimental import pallas as pl
from jax.experimental.pallas import tpu as pltpu

D_MODEL = 1024
N_HEADS = 16
HEAD_DIM = 64
LORA = 64
RMS_EPS = 1e-6
LNX_EPS = 64e-5
GRID_W = 64
WIN_ROWS = 8
WIN_COLS = 16
NEG_INF = -1e30

LANE_TILE = 128
HEADS_PER_TILE = LANE_TILE // HEAD_DIM
LANES_PER_GROUP = 256
HEADS_PER_GROUP = LANES_PER_GROUP // HEAD_DIM
N_GROUPS = D_MODEL // LANES_PER_GROUP
CHUNK = 64
TIME_TILE = 512
ROW_SUB = TIME_TILE
WKV_TILE = 512
WKV_LANES = LANES_PER_GROUP
WKV_SUB = 128
Q_ROWS = 4
VMEM_LIMIT = 56 * 1024 * 1024

F32 = jnp.float32
BF16 = jnp.bfloat16


def _dot(a, b):
    return jnp.dot(a, b, preferred_element_type=F32)


def _dot_nt(a, b):
    return lax.dot_general(a, b, (((1,), (1,)), ((), ())), preferred_element_type=F32)


def _dot_tn(a, b):
    return lax.dot_general(a, b, (((0,), (0,)), ((), ())), preferred_element_type=F32)


def _split_dot(x, ones_b):
    hi = x.astype(BF16)
    lo = (x - hi.astype(F32)).astype(BF16)
    return _dot(hi, ones_b) + _dot(lo, ones_b)


def _split_dot_left(ones_b, x):
    hi = x.astype(BF16)
    lo = (x - hi.astype(F32)).astype(BF16)
    return _dot(ones_b, hi) + _dot(ones_b, lo)


def _head_sum(x, ones_b, split):
    outs = []
    for g in range(x.shape[-1] // LANES_PER_GROUP):
        xs = x[:, g * LANES_PER_GROUP:(g + 1) * LANES_PER_GROUP]
        outs.append(_split_dot(xs, ones_b) if split else _dot(xs.astype(BF16), ones_b))
    return jnp.concatenate(outs, axis=-1) if len(outs) > 1 else outs[0]


def _rms(x, g):
    return x * lax.rsqrt(jnp.mean(x * x, axis=-1, keepdims=True) + RMS_EPS) * g


def _silu(z):
    return z * (1.0 / (1.0 + jnp.exp(-z)))


def _const_spec(shape):
    nd = len(shape)
    return pl.BlockSpec(shape, lambda *_: (0,) * nd, pipeline_mode=pl.Buffered(1))


def _rwkv_pre_kernel(x_ref, xp_ref, xq_ref, g_ref, mu_ref, wr_ref, wk_ref, wv_ref, wz_ref,
                     w0_ref, wd_ref, w2_ref, a0_ref, a2_ref, kkp_ref, ones_ref,
                     r_out, k_out, v_out, z_out, kk_out, as0_out, as1_out, lw0_out, lw1_out):
    i = pl.program_id(1)
    nt = pl.num_programs(1)
    tt = x_ref.shape[1]
    g = g_ref[...]
    n_sub = tt // ROW_SUB
    row = lax.broadcasted_iota(jnp.int32, (ROW_SUB, 1), 0)
    prev_row = jnp.where(i == 0, 0.0, _rms(xp_ref[0], g)[7:8])
    for s in range(n_sub):
        rows = slice(s * ROW_SUB, (s + 1) * ROW_SUB)
        xn = _rms(x_ref[0, rows, :], g)
        if s == n_sub - 1:
            next_row = jnp.where(i == nt - 1, 0.0, _rms(xq_ref[0], g)[0:1])
        else:
            next_row = _rms(x_ref[0, (s + 1) * ROW_SUB:(s + 1) * ROW_SUB + 8, :], g)[0:1]
        x_prev = jnp.where(row == 0, prev_row, pltpu.roll(xn, 1, axis=0))
        x_next = jnp.where(row == ROW_SUB - 1, next_row, pltpu.roll(xn, ROW_SUB - 1, axis=0))
        xx = 0.5 * (x_prev + x_next) - xn
        prev_row = xn[ROW_SUB - 1:ROW_SUB]

        def lerp(j):
            return (xn + xx * mu_ref[j:j + 1, :]).astype(BF16)

        dm = x_ref.shape[2]
        xn_b = xn.astype(BF16)
        down = _dot(xn_b, wd_ref[:dm, :]) + _dot(xx.astype(BF16), wd_ref[dm:, :])
        hw = jnp.tanh(down[:, :2 * LORA]).astype(BF16)
        ha = down[:, 2 * LORA:].astype(BF16)

        def sigmoid(u):
            return 0.5 + 0.5 * jnp.tanh(0.5 * u)

        def log_decay(d):
            wl = w0_ref[d:d + 1, :] + _dot(hw, w2_ref[d])
            return (-float(np.exp(-0.5))) * sigmoid(wl)

        def iclr(d):
            return sigmoid(a0_ref[d:d + 1, :] + _dot(ha, a2_ref[d])).astype(BF16)

        r_out[0, rows, :] = _dot(lerp(0), wr_ref[...]).astype(BF16)
        lw0_out[0, rows, :] = log_decay(0)
        k = _dot(lerp(1), wk_ref[...])
        k_out[0, rows, :] = k.astype(BF16)
        as0_out[0, rows, :] = iclr(0)
        v_out[0, rows, :] = _dot(lerp(2), wv_ref[...]).astype(BF16)
        lw1_out[0, rows, :] = log_decay(1)
        z_out[0, rows, :] = _dot(xn_b, wz_ref[...]).astype(BF16)
        as1_out[0, rows, :] = iclr(1)

        kk = k * kkp_ref[...]
        ss = _head_sum(kk * kk, ones_ref[...], False)
        kk_out[0, rows, :] = (kk * lax.rsqrt(jnp.maximum(ss, 1e-24))).astype(BF16)


def _rwkv_pre(x, g, mu, wr, wk, wv, wz, w0, wd, w2, a0, a2, kkp, ones_b):
    bsz, t, d = x.shape
    tt = TIME_TILE
    nt = t // tt
    rb = tt // 8
    tile = pl.BlockSpec((1, tt, d), lambda b, i: (b, i, 0))
    prev = pl.BlockSpec((1, 8, d), lambda b, i: (b, jnp.maximum(i * rb - 1, 0), 0))
    nxt = pl.BlockSpec((1, 8, d), lambda b, i: (b, jnp.minimum((i + 1) * rb, t // 8 - 1), 0))
    ins = [tile, prev, nxt] + [_const_spec(a.shape) for a in
                               (g, mu, wr, wk, wv, wz, w0, wd, w2, a0, a2, kkp, ones_b)]
    bshape = jax.ShapeDtypeStruct((bsz, t, d), BF16)
    fshape = jax.ShapeDtypeStruct((bsz, t, d), F32)
    return pl.pallas_call(
        _rwkv_pre_kernel,
        grid=(bsz, nt),
        in_specs=ins,
        out_specs=[tile] * 9,
        out_shape=[bshape] * 7 + [fshape] * 2,
        compiler_params=pltpu.CompilerParams(
            dimension_semantics=("parallel", "parallel"), vmem_limit_bytes=VMEM_LIMIT),
        name="rwkv_pre",
    )(x, x, x, g, mu, wr, wk, wv, wz, w0, wd, w2, a0, a2, kkp, ones_b)


def _head_lane_masks(rows, dtype):
    lane = lax.broadcasted_iota(jnp.int32, (rows, LANE_TILE), 1)
    return [jnp.where((lane // HEAD_DIM) == j, 1.0, 0.0).astype(dtype) for j in range(HEADS_PER_TILE)]


def _bd(x, masks):
    zero = jnp.zeros((x.shape[0], LANE_TILE), x.dtype)
    n_tiles = x.shape[1] // LANE_TILE
    blocks = []
    for h in range(x.shape[1] // HEAD_DIM):
        lt, j = divmod(h, HEADS_PER_TILE)
        piece = x[:, lt * LANE_TILE:(lt + 1) * LANE_TILE] * masks[j]
        blocks.append(jnp.concatenate([piece if t == lt else zero for t in range(n_tiles)], axis=1))
    return jnp.concatenate(blocks, axis=0)


def _block_transpose(x):
    tiles = []
    for lt in range(x.shape[1] // LANE_TILE):
        t = x[:, lt * LANE_TILE:(lt + 1) * LANE_TILE].T
        tiles.append(jnp.concatenate([t[j * HEAD_DIM:(j + 1) * HEAD_DIM] for j in range(HEADS_PER_TILE)], axis=1))
    return jnp.concatenate(tiles, axis=1)


def _wkv_kernel(rf_ref, kf_ref, vf_ref, kkf_ref, asf_ref, lwf_ref,
                rb_ref, kb_ref, vb_ref, kkb_ref, asb_ref, lwb_ref, ka_ref,
                yf_ref, yb_ref, sf_ref, sb_ref, pw_scr, rh_scr, qw_scr, yh_scr, *, nt):
    tt = rf_ref.shape[1]
    nch = tt // CHUNK
    gl = WKV_LANES
    step_id = pl.program_id(0)
    slot = step_id % 2
    prev = 1 - slot

    @pl.when(step_id == 0)
    def _():
        sf_ref[...] = jnp.zeros_like(sf_ref)
        sb_ref[...] = jnp.zeros_like(sb_ref)
        for scr in (pw_scr, rh_scr, qw_scr, yh_scr):
            scr[...] = jnp.zeros_like(scr)

    mb = _head_lane_masks(CHUNK, BF16)
    unit_of = {(di, ci): ci * 2 + di for ci in range(nch) for di in range(2)}

    first_tile = (jnp.maximum(step_id - 1, 0) % nt) == 0
    s_refs = (sf_ref, sb_ref)
    y_refs = (yf_ref, yb_ref)
    state = [jnp.where(first_tile, 0.0, ref[...]) for ref in s_refs]

    def seq_step(step):
        for di in range(2):
            ci = nch - 1 - step if di == 1 else step
            u = unit_of[(di, ci)]
            ys = _dot(jnp.concatenate([pw_scr[prev, u], rh_scr[prev, u]], axis=0),
                      _bd(state[di].astype(BF16), mb))
            state[di] = ys[:CHUNK] + qw_scr[prev, u]
            y_refs[di][0, ci * CHUNK:(ci + 1) * CHUNK, :] = ys[CHUNK:] + yh_scr[prev, u]

    tw = lax.broadcasted_iota(jnp.int32, (CHUNK, gl), 0)
    iw = lax.broadcasted_iota(jnp.int32, (CHUNK, gl), 1) % CHUNK
    eye_f = jnp.where(iw == tw, 1.0, 0.0)
    rt = lax.broadcasted_iota(jnp.int32, (WKV_SUB, WKV_SUB), 0)
    ct = lax.broadcasted_iota(jnp.int32, (WKV_SUB, WKV_SUB), 1)
    same = (rt // CHUNK) == (ct // CHUNK)
    tri = [jnp.where(same & (ct <= rt), 1.0, 0.0).astype(BF16), jnp.where(same & (ct >= rt), 1.0, 0.0).astype(BF16)]
    strict_of = [iw < tw, iw > tw]
    incl_of = [iw <= tw, iw >= tw]
    ka = ka_ref[...]
    in_refs = ((rf_ref, kf_ref, vf_ref, kkf_ref, asf_ref, lwf_ref), (rb_ref, kb_ref, vb_ref, kkb_ref, asb_ref, lwb_ref))

    units, c_tot, kt_w, bt_w, vb, a_t, r_t, strict, incl = [], [], [], [], [], [], [], [], []
    lab, lak, arb, ark, la, p = [], [], [], [], [], []
    n_sub = tt // WKV_SUB
    for w in range(n_sub):
        first = len(units)
        for di in range(2):
            r_ref, k_ref, v_ref, kk_ref, as_ref, lw_ref = in_refs[di]
            rows = slice(w * WKV_SUB, (w + 1) * WKV_SUB)
            kk = kk_ref[0, rows, :].astype(F32)
            asg = as_ref[0, rows, :].astype(F32)
            lw = lw_ref[0, rows, :]
            kd = k_ref[0, rows, :].astype(F32) * (1.0 + (asg - 1.0) * ka)
            bvec = kk * asg
            c_in = _split_dot_left(tri[di], lw)
            e_neg = jnp.exp(-c_in)
            rt_s = (r_ref[0, rows, :].astype(F32) * jnp.exp(c_in)).astype(BF16)
            at_s = (-kk * jnp.exp(c_in - lw)).astype(BF16)
            kt_s = (kd * e_neg).astype(BF16)
            bt_s = (bvec * e_neg).astype(BF16)
            for cs in range(WKV_SUB // CHUNK):
                sl = slice(cs * CHUNK, (cs + 1) * CHUNK)
                ci = w * (WKV_SUB // CHUNK) + cs
                last = cs * CHUNK if di == 1 else (cs + 1) * CHUNK - 1
                ct_u = c_in[last:last + 1, :]
                e_hat = jnp.exp(ct_u - c_in[sl])
                units.append((di, ci))
                c_tot.append(ct_u)
                kt_w.append(_block_transpose(kd[sl] * e_hat).astype(BF16))
                bt_w.append(_block_transpose(bvec[sl] * e_hat).astype(BF16))
                vb.append(v_ref[0, ci * CHUNK:(ci + 1) * CHUNK, :])
                a_t.append(at_s[sl])
                r_t.append(rt_s[sl])
                strict.append(strict_of[di])
                incl.append(incl_of[di])
                ar = jnp.concatenate([at_s[sl], rt_s[sl]], axis=0)
                bk = jnp.concatenate([_bd(bt_s[sl], mb), _bd(kt_s[sl], mb)], axis=0)
                lab.append(None)
                lak.append((ar, bk))
            seq_step(2 * w + di)
        for u in range(first, len(units)):
            ar, bk = lak[u]
            sbk = _dot_nt(ar, bk)
            lab[u] = jnp.where(strict[u], sbk[:CHUNK, :gl], 0.0)
            lak[u] = jnp.where(strict[u], sbk[:CHUNK, gl:], 0.0).astype(BF16)
            arb.append(jnp.where(incl[u], sbk[CHUNK:, :gl], 0.0).astype(BF16))
            ark.append(jnp.where(incl[u], sbk[CHUNK:, gl:], 0.0).astype(BF16))
        for u in range(first, len(units)):
            la.append(_dot(jnp.concatenate([lak[u], ark[u], kt_w[u]], axis=0), _bd(vb[u], mb)))
            lb = lab[u].astype(BF16)
            p.append(_dot(lb, _bd(lb, mb)))
    for di in range(2):
        s_refs[di][...] = state[di]
    n = range(len(units))

    x = [eye_f + l for l in lab]
    n_lvl = int(np.log2(CHUNK)) - 1
    for lvl in range(n_lvl):
        pb = [q.astype(BF16) for q in p]
        if lvl < n_lvl - 1:
            xp = [_dot(jnp.concatenate([x[u].astype(BF16), pb[u]], axis=0), _bd(pb[u], mb)) for u in n]
            x = [x[u] + xp[u][:CHUNK] for u in n]
            p = [q[CHUNK:] for q in xp]
        else:
            x = [x[u] + _dot(x[u].astype(BF16), _bd(pb[u], mb)) for u in n]
    tb = [q.astype(BF16) for q in x]

    au = [_dot(tb[u], jnp.concatenate([_bd(a_t[u], mb), _bd(la[u][:CHUNK].astype(BF16), mb)], axis=1))
          for u in n]
    au_b = [q.astype(BF16) for q in au]
    rq = [_dot(jnp.concatenate([arb[u], bt_w[u]], axis=0),
               jnp.concatenate([_bd(au_b[u][:, :gl], mb), _bd(au_b[u][:, gl:], mb)], axis=1)) for u in n]
    for u in n:
        key = unit_of[units[u]]
        rh_scr[slot, key] = (r_t[u].astype(F32) + rq[u][:CHUNK, :gl]).astype(BF16)
        yh_scr[slot, key] = rq[u][:CHUNK, gl:] + la[u][CHUNK:2 * CHUNK]
        pw_scr[slot, key] = (rq[u][CHUNK:, :gl] + eye_f * jnp.exp(c_tot[u])).astype(BF16)
        qw_scr[slot, key] = rq[u][CHUNK:, gl:] + la[u][2 * CHUNK:]


def _wkv(r, k, v, kk, as0, lw0, as1, lw1, ka):
    bsz, t, d = r.shape
    tt = WKV_TILE
    nt = t // tt
    gl = WKV_LANES
    ng = d // gl
    n_items = bsz * ng * nt

    def item(m):
        return m // (ng * nt), (m // nt) % ng, m % nt

    def in_map(reverse):
        def index(s):
            b, g, i = item(jnp.minimum(s, n_items - 1))
            return b, (nt - 1 - i) if reverse else i, g
        return index

    def out_map(reverse):
        def index(s):
            b, g, i = item(jnp.maximum(s - 1, 0))
            return b, (nt - 1 - i) if reverse else i, g
        return index

    tile = (1, tt, gl)
    fwd, bwd = pl.BlockSpec(tile, in_map(False)), pl.BlockSpec(tile, in_map(True))
    ka_spec = pl.BlockSpec((1, gl), lambda s: (0, item(jnp.minimum(s, n_items - 1))[1]))
    yshape = jax.ShapeDtypeStruct((bsz, t, d), F32)
    n_units = 2 * (tt // CHUNK)
    return pl.pallas_call(
        functools.partial(_wkv_kernel, nt=nt),
        grid=(n_items + 1,),
        in_specs=[fwd] * 6 + [bwd] * 6 + [ka_spec],
        out_specs=[pl.BlockSpec(tile, out_map(False)), pl.BlockSpec(tile, out_map(True))],
        out_shape=[yshape, yshape],
        scratch_shapes=[pltpu.VMEM((HEAD_DIM, gl), F32), pltpu.VMEM((HEAD_DIM, gl), F32),
                        pltpu.VMEM((2, n_units, CHUNK, gl), BF16), pltpu.VMEM((2, n_units, CHUNK, gl), BF16),
                        pltpu.VMEM((2, n_units, CHUNK, gl), F32), pltpu.VMEM((2, n_units, CHUNK, gl), F32)],
        compiler_params=pltpu.CompilerParams(
            dimension_semantics=("arbitrary",), vmem_limit_bytes=VMEM_LIMIT),
        name="wkv",
    )(r, k, v, kk, as0, lw0, r, k, v, kk, as1, lw1, ka)


def _rwkv_post_kernel(yf_ref, yb_ref, r_ref, k_ref, v_ref, as0_ref, as1_ref, z_ref, x_ref,
                      ka_ref, rk_ref, lnw_ref, lnb_ref, wo_ref, pg_ref, ng_ref, win_ref, bin_ref,
                      ones_ref, x1_out, q_out, k_out, v_out, z_out):
    ones_b = ones_ref[...]
    ka = ka_ref[...]
    e = D_MODEL
    y = yf_ref[0] + yb_ref[0]
    mean = _head_sum(y, ones_b, True) * (1.0 / HEAD_DIM)
    yc = y - mean
    var = _head_sum(yc * yc, ones_b, False) * (1.0 / HEAD_DIM)
    yn = yc * lax.rsqrt(var + LNX_EPS) * lnw_ref[...] + lnb_ref[...]
    r = r_ref[0].astype(F32)
    k = k_ref[0].astype(F32)
    v = v_ref[0].astype(F32)
    a_sum = as0_ref[0].astype(F32) + as1_ref[0].astype(F32)
    kd_sum = k * (2.0 + (a_sum - 2.0) * ka)
    bonus = _head_sum(r * kd_sum * rk_ref[...], ones_b, False) * v
    out = (yn + bonus) * _silu(z_ref[0].astype(F32))
    h = _dot(out.astype(BF16), wo_ref[...])
    x1 = x_ref[0] + _rms(h, pg_ref[...])
    x1_out[0] = x1
    xn = _rms(x1, ng_ref[...]).astype(BF16)
    for j, o_ref in enumerate((q_out, k_out, v_out, z_out)):
        pj = _dot(xn, win_ref[:, j * e:(j + 1) * e]) + bin_ref[:, j * e:(j + 1) * e]
        if j == 0:
            pj = pj * (HEAD_DIM ** -0.5)
        o_ref[0] = pj.astype(BF16)


def _rwkv_post(yf, yb, r, k, v, as0, as1, z, x, ka, rk, lnw, lnb, wo, pg, ng, win, b_in, ones_b):
    bsz, t, d = x.shape
    tt = TIME_TILE
    tile = pl.BlockSpec((1, tt, d), lambda b, i: (b, i, 0))
    consts = (ka, rk, lnw, lnb, wo, pg, ng, win, b_in, ones_b)
    bshape = jax.ShapeDtypeStruct((bsz, t, d), BF16)
    return pl.pallas_call(
        _rwkv_post_kernel,
        grid=(bsz, t // tt),
        in_specs=[tile] * 9 + [_const_spec(a.shape) for a in consts],
        out_specs=[tile] * 5,
        out_shape=[jax.ShapeDtypeStruct((bsz, t, d), F32)] + [bshape] * 4,
        compiler_params=pltpu.CompilerParams(
            dimension_semantics=("parallel", "parallel"), vmem_limit_bytes=VMEM_LIMIT),
        name="rwkv_post",
    )(yf, yb, r, k, v, as0, as1, z, x, *consts)


def _natten_kernel(q_ref, k_ref, v_ref, z_ref, x_ref, bias_ref, wo_ref, bo_ref, pg_ref,
                   out_ref, o_scr, *, rows):
    step = pl.program_id(1)
    gl = LANES_PER_GROUP
    kwin = WIN_ROWS * GRID_W
    r2 = lax.broadcasted_iota(jnp.int32, (gl, gl), 0)
    c2 = lax.broadcasted_iota(jnp.int32, (gl, gl), 1)
    bdmask = (r2 // HEAD_DIM) == (c2 // HEAD_DIM)

    for j in range(Q_ROWS):
        row = step * Q_ROWS + j
        rs = jnp.clip(row - WIN_ROWS // 2, 0, rows - WIN_ROWS)
        dr0 = rs - row + WIN_ROWS - 1
        k0 = pl.multiple_of(rs * GRID_W, GRID_W)
        qs = slice(j * GRID_W, (j + 1) * GRID_W)
        for g in range(N_GROUPS):
            gs = slice(g * gl, (g + 1) * gl)
            qb = q_ref[0, qs, gs]
            qst = jnp.where(bdmask, jnp.concatenate([qb] * HEADS_PER_GROUP, axis=0), jnp.zeros((), BF16))
            kw = k_ref[0, pl.ds(k0, kwin), gs]
            vw = v_ref[0, pl.ds(k0, kwin), gs]
            s = _dot_nt(qst, kw)
            bias = jnp.concatenate([bias_ref[g, dr0 + 2 * p] for p in range(WIN_ROWS // 2)], axis=-1)
            s = s + bias
            m = jnp.max(s, axis=-1, keepdims=True)
            p = jnp.exp(s - m)
            l = jnp.sum(p, axis=-1, keepdims=True)
            pv = _dot(p.astype(BF16), vw) * (1.0 / l)
            pv = jnp.where(bdmask, pv, 0.0)
            o = pv[0:GRID_W]
            for h in range(1, HEADS_PER_GROUP):
                o = o + pv[h * GRID_W:(h + 1) * GRID_W]
            o_scr[qs, gs] = o

    gated = o_scr[...] * _silu(z_ref[0].astype(F32))
    h = _dot(gated.astype(BF16), wo_ref[...]) + bo_ref[...]
    out_ref[0] = x_ref[0] + _rms(h, pg_ref[...])


def _natten(q, k, v, z, x1, bias_tab, wo, bo, pg):
    bsz, t, d = x1.shape
    rows = t // GRID_W
    tq = Q_ROWS * GRID_W
    tile = pl.BlockSpec((1, tq, d), lambda b, i: (b, i, 0))
    full = pl.BlockSpec((1, t, d), lambda b, i: (b, 0, 0))
    consts = (bias_tab, wo, bo, pg)
    return pl.pallas_call(
        functools.partial(_natten_kernel, rows=rows),
        grid=(bsz, rows // Q_ROWS),
        in_specs=[tile, full, full, tile, tile] + [_const_spec(a.shape) for a in consts],
        out_specs=tile,
        out_shape=jax.ShapeDtypeStruct((bsz, t, d), F32),
        scratch_shapes=[pltpu.VMEM((tq, d), F32)],
        compiler_params=pltpu.CompilerParams(
            dimension_semantics=("parallel", "arbitrary"), vmem_limit_bytes=VMEM_LIMIT),
        name="natten",
    )(q, k, v, z, x1, *consts)


def _bias_table(rpb):
    qc = np.arange(GRID_W)[:, None]
    kc = np.arange(GRID_W)[None, :]
    ws = np.clip(qc - WIN_COLS // 2, 0, GRID_W - WIN_COLS)
    valid = (kc >= ws) & (kc < ws + WIN_COLS)
    rel = np.clip(kc - qc + WIN_COLS - 1, 0, 2 * WIN_COLS - 2)
    tab = jnp.where(jnp.asarray(valid)[None, None], rpb[:, :, rel], NEG_INF)
    two = jnp.concatenate([tab[:, :-1], tab[:, 1:]], axis=-1)
    two = two.reshape(N_GROUPS, HEADS_PER_GROUP, 2 * WIN_ROWS - 2, GRID_W, 2 * GRID_W)
    return jnp.transpose(two, (0, 2, 1, 3, 4)).reshape(
        N_GROUPS, 2 * WIN_ROWS - 2, HEADS_PER_GROUP * GRID_W, 2 * GRID_W).astype(F32)


def _trunk(x, p):
    row = lambda a: a.reshape(1, -1)
    r, k, v, z, kk, as0, as1, lw0, lw1 = _rwkv_pre(
        x, row(p["pre_g"][0]), p["mu"], p["w_r"], p["w_k"], p["w_v"], p["w_z"], p["w0"], p["w_down"], p["w2"],
        p["a0"], p["a2"], row(p["k_k"]), p["ones"])
    ka = row(p["k_a"])
    yf, yb = _wkv(r, k, v, kk, as0, lw0, as1, lw1, ka)
    x1, q, k2, v2, z2 = _rwkv_post(
        yf, yb, r, k, v, as0, as1, z, x, ka, row(p["r_k"]), row(p["lnx_w"]), row(p["lnx_b"]), p["rk_w_o"],
        row(p["post_g"][0]), row(p["pre_g"][1]), p["w_in"], row(p["b_in"]), p["ones"])
    return _natten(q, k2, v2, z2, x1, p["bias_tab"], p["na_w_o"], row(p["b_o"]), row(p["post_g"][1]))


def kernel(x_prompt, x_sample, pre_norm_g, post_norm_g, rk_mu, rk_w_r, rk_w_k, rk_w_v, rk_w_z, rk_w0, rk_w1, rk_w2, rk_a0, rk_a1, rk_a2, rk_k_k, rk_k_a, rk_r_k, rk_lnx_w, rk_lnx_b, rk_w_o, na_w_in, na_b_in, na_rpb, na_w_o, na_b_o):
    bf = lambda a: a.astype(BF16)
    hd = np.arange(LANES_PER_GROUP) // HEAD_DIM
    downs = (rk_w1[0, 0], rk_w1[0, 1], rk_a1[0, 0], rk_a1[0, 1])
    lora_down = jnp.concatenate(downs, axis=1)
    lora_down_mu = jnp.concatenate([rk_mu[0, 3 + j][:, None] * w for j, w in enumerate(downs)], axis=1)
    zpad = jnp.zeros((LORA, D_MODEL), F32)
    p = dict(
        pre_g=pre_norm_g, post_g=post_norm_g, mu=rk_mu[0],
        w_r=bf(rk_w_r[0]), w_k=bf(rk_w_k[0]), w_v=bf(rk_w_v[0]), w_z=bf(rk_w_z[0]),
        w0=rk_w0[0], a0=rk_a0[0], w_down=bf(jnp.concatenate([lora_down, lora_down_mu], axis=0)),
        w2=bf(jnp.stack([jnp.concatenate([rk_w2[0, 0], zpad], axis=0), jnp.concatenate([zpad, rk_w2[0, 1]], axis=0)])),
        a2=bf(jnp.stack([jnp.concatenate([rk_a2[0, 0], zpad], axis=0), jnp.concatenate([zpad, rk_a2[0, 1]], axis=0)])),
        k_k=rk_k_k[0], k_a=rk_k_a[0], r_k=rk_r_k[0], lnx_w=rk_lnx_w[0], lnx_b=rk_lnx_b[0],
        rk_w_o=bf(rk_w_o[0]), w_in=bf(na_w_in[0]), b_in=na_b_in[0], na_w_o=bf(na_w_o[0]), b_o=na_b_o[0],
        bias_tab=_bias_table(na_rpb[0]),
        ones=jnp.asarray(hd[:, None] == hd[None, :], dtype=BF16),
    )
    return (_trunk(x_prompt, p), _trunk(x_sample, p))
```

```python
import functools

import jax
import jax.numpy as jnp
import numpy as np
from jax import lax
from jax.experimental import pallas as pl
from jax.experimental.pallas import tpu as pltpu

D_MODEL = 1024
N_HEADS = 16
HEAD_DIM = 64
LORA = 64
RMS_EPS = 1e-6
LNX_EPS = 64e-5
GRID_W = 64
WIN_ROWS = 8
WIN_COLS = 16
NEG_INF = -1e30
LOG2E = float(np.log2(np.e))

LANE_TILE = 128
HEADS_PER_TILE = LANE_TILE // HEAD_DIM
LANES_PER_GROUP = 256
HEADS_PER_GROUP = LANES_PER_GROUP // HEAD_DIM
N_GROUPS = D_MODEL // LANES_PER_GROUP
CHUNK = 64
TIME_TILE = 512
ROW_SUB = TIME_TILE
WKV_TILE = 1024
WKV_LANES = LANES_PER_GROUP
WKV_SUB = 128
Q_ROWS = 8
VMEM_LIMIT = 56 * 1024 * 1024

F32 = jnp.float32
BF16 = jnp.bfloat16


def _dot(a, b):
    return jnp.dot(a, b, preferred_element_type=F32)


def _dot_nt(a, b):
    return lax.dot_general(a, b, (((1,), (1,)), ((), ())), preferred_element_type=F32)


def _dot_tn(a, b):
    return lax.dot_general(a, b, (((0,), (0,)), ((), ())), preferred_element_type=F32)


def _split_dot(x, ones_b):
    hi = x.astype(BF16)
    lo = (x - hi.astype(F32)).astype(BF16)
    return _dot(hi, ones_b) + _dot(lo, ones_b)


def _split_dot_left(ones_b, x):
    hi = x.astype(BF16)
    lo = (x - hi.astype(F32)).astype(BF16)
    return _dot(ones_b, hi) + _dot(ones_b, lo)


def _head_sum(x, ones_b, split):
    outs = []
    for g in range(x.shape[-1] // LANES_PER_GROUP):
        xs = x[:, g * LANES_PER_GROUP:(g + 1) * LANES_PER_GROUP]
        outs.append(_split_dot(xs, ones_b) if split else _dot(xs.astype(BF16), ones_b))
    return jnp.concatenate(outs, axis=-1) if len(outs) > 1 else outs[0]


def _rms(x, g):
    return x * lax.rsqrt(jnp.mean(x * x, axis=-1, keepdims=True) + RMS_EPS) * g


def _silu(z):
    return z * (1.0 / (1.0 + jnp.exp(-z)))


def _const_spec(shape):
    nd = len(shape)
    return pl.BlockSpec(shape, lambda *_: (0,) * nd, pipeline_mode=pl.Buffered(1))


def _rwkv_pre_kernel(x_ref, xp_ref, xq_ref, g_ref, mu_ref, wr_ref, wk_ref, wv_ref, wz_ref,
                     w0_ref, wd_ref, w2_ref, a0_ref, a2_ref, kkp_ref, ones_ref,
                     r_out, k_out, v_out, z_out, kk_out, as0_out, as1_out, lw0_out, lw1_out):
    i = pl.program_id(1)
    nt = pl.num_programs(1)
    tt = x_ref.shape[1]
    g = g_ref[...]
    n_sub = tt // ROW_SUB
    row = lax.broadcasted_iota(jnp.int32, (ROW_SUB, 1), 0)
    prev_row = jnp.where(i == 0, 0.0, _rms(xp_ref[0], g)[7:8])
    for s in range(n_sub):
        rows = slice(s * ROW_SUB, (s + 1) * ROW_SUB)
        xn = _rms(x_ref[0, rows, :], g)
        if s == n_sub - 1:
            next_row = jnp.where(i == nt - 1, 0.0, _rms(xq_ref[0], g)[0:1])
        else:
            next_row = _rms(x_ref[0, (s + 1) * ROW_SUB:(s + 1) * ROW_SUB + 8, :], g)[0:1]
        x_prev = jnp.where(row == 0, prev_row, pltpu.roll(xn, 1, axis=0))
        x_next = jnp.where(row == ROW_SUB - 1, next_row, pltpu.roll(xn, ROW_SUB - 1, axis=0))
        xx = 0.5 * (x_prev + x_next) - xn
        prev_row = xn[ROW_SUB - 1:ROW_SUB]

        def lerp(j):
            return (xn + xx * mu_ref[j:j + 1, :]).astype(BF16)

        dm = x_ref.shape[2]
        xn_b = xn.astype(BF16)
        down = _dot(xn_b, wd_ref[:dm, :]) + _dot(xx.astype(BF16), wd_ref[dm:, :])
        hw = jnp.tanh(down[:, :2 * LORA]).astype(BF16)
        ha = down[:, 2 * LORA:].astype(BF16)

        def sigmoid(u):
            return 0.5 + 0.5 * jnp.tanh(0.5 * u)

        def log_decay(d):
            wl = w0_ref[d:d + 1, :] + _dot(hw, w2_ref[d])
            return (-float(np.exp(-0.5))) * sigmoid(wl)

        def iclr(d):
            return sigmoid(a0_ref[d:d + 1, :] + _dot(ha, a2_ref[d])).astype(BF16)

        r_out[0, rows, :] = _dot(lerp(0), wr_ref[...]).astype(BF16)
        lw0_out[0, rows, :] = log_decay(0)
        k = _dot(lerp(1), wk_ref[...])
        k_out[0, rows, :] = k.astype(BF16)
        as0_out[0, rows, :] = iclr(0)
        v_out[0, rows, :] = _dot(lerp(2), wv_ref[...]).astype(BF16)
        lw1_out[0, rows, :] = log_decay(1)
        z_out[0, rows, :] = _dot(xn_b, wz_ref[...]).astype(BF16)
        as1_out[0, rows, :] = iclr(1)

        kk = k * kkp_ref[...]
        ss = _head_sum(kk * kk, ones_ref[...], False)
        kk_out[0, rows, :] = (kk * lax.rsqrt(jnp.maximum(ss, 1e-24))).astype(BF16)


def _rwkv_pre(x, g, mu, wr, wk, wv, wz, w0, wd, w2, a0, a2, kkp, ones_b):
    bsz, t, d = x.shape
    tt = TIME_TILE
    nt = t // tt
    rb = tt // 8
    tile = pl.BlockSpec((1, tt, d), lambda b, i: (b, i, 0))
    prev = pl.BlockSpec((1, 8, d), lambda b, i: (b, jnp.maximum(i * rb - 1, 0), 0))
    nxt = pl.BlockSpec((1, 8, d), lambda b, i: (b, jnp.minimum((i + 1) * rb, t // 8 - 1), 0))
    ins = [tile, prev, nxt] + [_const_spec(a.shape) for a in
                               (g, mu, wr, wk, wv, wz, w0, wd, w2, a0, a2, kkp, ones_b)]
    bshape = jax.ShapeDtypeStruct((bsz, t, d), BF16)
    fshape = jax.ShapeDtypeStruct((bsz, t, d), F32)
    return pl.pallas_call(
        _rwkv_pre_kernel,
        grid=(bsz, nt),
        in_specs=ins,
        out_specs=[tile] * 9,
        out_shape=[bshape] * 7 + [fshape] * 2,
        compiler_params=pltpu.CompilerParams(
            dimension_semantics=("parallel", "parallel"), vmem_limit_bytes=VMEM_LIMIT),
        name="rwkv_pre",
    )(x, x, x, g, mu, wr, wk, wv, wz, w0, wd, w2, a0, a2, kkp, ones_b)


def _head_lane_masks(rows, dtype):
    lane = lax.broadcasted_iota(jnp.int32, (rows, LANE_TILE), 1)
    return [jnp.where((lane // HEAD_DIM) == j, 1.0, 0.0).astype(dtype) for j in range(HEADS_PER_TILE)]


def _bd(x, masks):
    zero = jnp.zeros((x.shape[0], LANE_TILE), x.dtype)
    n_tiles = x.shape[1] // LANE_TILE
    blocks = []
    for h in range(x.shape[1] // HEAD_DIM):
        lt, j = divmod(h, HEADS_PER_TILE)
        piece = x[:, lt * LANE_TILE:(lt + 1) * LANE_TILE] * masks[j]
        blocks.append(jnp.concatenate([piece if t == lt else zero for t in range(n_tiles)], axis=1))
    return jnp.concatenate(blocks, axis=0)


def _block_transpose(x):
    tiles = []
    for lt in range(x.shape[1] // LANE_TILE):
        t = x[:, lt * LANE_TILE:(lt + 1) * LANE_TILE].T
        tiles.append(jnp.concatenate([t[j * HEAD_DIM:(j + 1) * HEAD_DIM] for j in range(HEADS_PER_TILE)], axis=1))
    return jnp.concatenate(tiles, axis=1)


def _wkv_kernel(rf_ref, kf_ref, vf_ref, kkf_ref, asf_ref, lwf_ref,
                rb_ref, kb_ref, vb_ref, kkb_ref, asb_ref, lwb_ref, ka_ref,
                yf_ref, yb_ref, sf_ref, sb_ref, pw_scr, rh_scr, qw_scr, yh_scr, *, nt):
    tt = rf_ref.shape[1]
    nch = tt // CHUNK
    gl = WKV_LANES
    step_id = pl.program_id(0)
    slot = step_id % 2
    prev = 1 - slot

    @pl.when(step_id == 0)
    def _():
        sf_ref[...] = jnp.zeros_like(sf_ref)
        sb_ref[...] = jnp.zeros_like(sb_ref)
        for scr in (pw_scr, rh_scr, qw_scr, yh_scr):
            scr[...] = jnp.zeros_like(scr)

    mb = _head_lane_masks(CHUNK, BF16)
    unit_of = {(di, ci): ci * 2 + di for ci in range(nch) for di in range(2)}

    first_tile = (jnp.maximum(step_id - 1, 0) % nt) == 0
    s_refs = (sf_ref, sb_ref)
    y_refs = (yf_ref, yb_ref)
    state = [jnp.where(first_tile, 0.0, ref[...]) for ref in s_refs]

    def seq_step(step):
        for di in range(2):
            ci = nch - 1 - step if di == 1 else step
            u = unit_of[(di, ci)]
            ys = _dot(jnp.concatenate([pw_scr[prev, u], rh_scr[prev, u]], axis=0),
                      _bd(state[di].astype(BF16), mb))
            state[di] = ys[:CHUNK] + qw_scr[prev, u]
            y_refs[di][0, ci * CHUNK:(ci + 1) * CHUNK, :] = ys[CHUNK:] + yh_scr[prev, u]

    tw = lax.broadcasted_iota(jnp.int32, (CHUNK, gl), 0)
    iw = lax.broadcasted_iota(jnp.int32, (CHUNK, gl), 1) % CHUNK
    eye_f = jnp.where(iw == tw, 1.0, 0.0)
    rt = lax.broadcasted_iota(jnp.int32, (WKV_SUB, WKV_SUB), 0)
    ct = lax.broadcasted_iota(jnp.int32, (WKV_SUB, WKV_SUB), 1)
    same = (rt // CHUNK) == (ct // CHUNK)
    tri = [jnp.where(same & (ct <= rt), 1.0, 0.0).astype(BF16), jnp.where(same & (ct >= rt), 1.0, 0.0).astype(BF16)]
    strict_of = [iw < tw, iw > tw]
    incl_of = [iw <= tw, iw >= tw]
    ka = ka_ref[...]
    in_refs = ((rf_ref, kf_ref, vf_ref, kkf_ref, asf_ref, lwf_ref), (rb_ref, kb_ref, vb_ref, kkb_ref, asb_ref, lwb_ref))

    units, c_tot, kt_w, bt_w, vb, a_t, r_t, strict, incl = [], [], [], [], [], [], [], [], []
    lab, lak, arb, ark, la, p = [], [], [], [], [], []
    n_sub = tt // WKV_SUB
    for w in range(n_sub):
        first = len(units)
        for di in range(2):
            r_ref, k_ref, v_ref, kk_ref, as_ref, lw_ref = in_refs[di]
            rows = slice(w * WKV_SUB, (w + 1) * WKV_SUB)
            kk = kk_ref[0, rows, :].astype(F32)
            asg = as_ref[0, rows, :].astype(F32)
            lw = lw_ref[0, rows, :]
            kd = k_ref[0, rows, :].astype(F32) * (1.0 + (asg - 1.0) * ka)
            bvec = kk * asg
            c_in = _split_dot_left(tri[di], lw)
            e_neg = jnp.exp(-c_in)
            rt_s = (r_ref[0, rows, :].astype(F32) * jnp.exp(c_in)).astype(BF16)
            at_s = (-kk * jnp.exp(c_in - lw)).astype(BF16)
            kt_s = (kd * e_neg).astype(BF16)
            bt_s = (bvec * e_neg).astype(BF16)
            for cs in range(WKV_SUB // CHUNK):
                sl = slice(cs * CHUNK, (cs + 1) * CHUNK)
                ci = w * (WKV_SUB // CHUNK) + cs
                last = cs * CHUNK if di == 1 else (cs + 1) * CHUNK - 1
                ct_u = c_in[last:last + 1, :]
                e_hat = jnp.exp(ct_u - c_in[sl])
                units.append((di, ci))
                c_tot.append(ct_u)
                kt_w.append(_block_transpose(kd[sl] * e_hat).astype(BF16))
                bt_w.append(_block_transpose(bvec[sl] * e_hat).astype(BF16))
                vb.append(v_ref[0, ci * CHUNK:(ci + 1) * CHUNK, :])
                a_t.append(at_s[sl])
                r_t.append(rt_s[sl])
                strict.append(strict_of[di])
                incl.append(incl_of[di])
                ar = jnp.concatenate([at_s[sl], rt_s[sl]], axis=0)
                bk = jnp.concatenate([_bd(bt_s[sl], mb), _bd(kt_s[sl], mb)], axis=0)
                lab.append(None)
                lak.append((ar, bk))
            seq_step(2 * w + di)
        for u in range(first, len(units)):
            ar, bk = lak[u]
            sbk = _dot_nt(ar, bk)
            lab[u] = jnp.where(strict[u], sbk[:CHUNK, :gl], 0.0)
            lak[u] = jnp.where(strict[u], sbk[:CHUNK, gl:], 0.0).astype(BF16)
            arb.append(jnp.where(incl[u], sbk[CHUNK:, :gl], 0.0).astype(BF16))
            ark.append(jnp.where(incl[u], sbk[CHUNK:, gl:], 0.0).astype(BF16))
        for u in range(first, len(units)):
            la.append(_dot(jnp.concatenate([lak[u], ark[u], kt_w[u]], axis=0), _bd(vb[u], mb)))
            lb = lab[u].astype(BF16)
            p.append(_dot(lb, _bd(lb, mb)))
    for di in range(2):
        s_refs[di][...] = state[di]
    n = range(len(units))

    x = [eye_f + l for l in lab]
    n_lvl = int(np.log2(CHUNK)) - 1
    for lvl in range(n_lvl):
        pb = [q.astype(BF16) for q in p]
        if lvl < n_lvl - 1:
            xp = [_dot(jnp.concatenate([x[u].astype(BF16), pb[u]], axis=0), _bd(pb[u], mb)) for u in n]
            x = [x[u] + xp[u][:CHUNK] for u in n]
            p = [q[CHUNK:] for q in xp]
        else:
            x = [x[u] + _dot(x[u].astype(BF16), _bd(pb[u], mb)) for u in n]
    tb = [q.astype(BF16) for q in x]

    au = [_dot(tb[u], jnp.concatenate([_bd(a_t[u], mb), _bd(la[u][:CHUNK].astype(BF16), mb)], axis=1))
          for u in n]
    au_b = [q.astype(BF16) for q in au]
    rq = [_dot(jnp.concatenate([arb[u], bt_w[u]], axis=0),
               jnp.concatenate([_bd(au_b[u][:, :gl], mb), _bd(au_b[u][:, gl:], mb)], axis=1)) for u in n]
    for u in n:
        key = unit_of[units[u]]
        rh_scr[slot, key] = (r_t[u].astype(F32) + rq[u][:CHUNK, :gl]).astype(BF16)
        yh_scr[slot, key] = rq[u][:CHUNK, gl:] + la[u][CHUNK:2 * CHUNK]
        pw_scr[slot, key] = (rq[u][CHUNK:, :gl] + eye_f * jnp.exp(c_tot[u])).astype(BF16)
        qw_scr[slot, key] = rq[u][CHUNK:, gl:] + la[u][2 * CHUNK:]


def _wkv(r, k, v, kk, as0, lw0, as1, lw1, ka):
    bsz, t, d = r.shape
    tt = WKV_TILE
    nt = t // tt
    gl = WKV_LANES
    ng = d // gl
    n_items = bsz * ng * nt

    def item(m):
        return m // (ng * nt), (m // nt) % ng, m % nt

    def in_map(reverse):
        def index(s):
            b, g, i = item(jnp.minimum(s, n_items - 1))
            return b, (nt - 1 - i) if reverse else i, g
        return index

    def out_map(reverse):
        def index(s):
            b, g, i = item(jnp.maximum(s - 1, 0))
            return b, (nt - 1 - i) if reverse else i, g
        return index

    tile = (1, tt, gl)
    fwd, bwd = pl.BlockSpec(tile, in_map(False)), pl.BlockSpec(tile, in_map(True))
    ka_spec = pl.BlockSpec((1, gl), lambda s: (0, item(jnp.minimum(s, n_items - 1))[1]))
    yshape = jax.ShapeDtypeStruct((bsz, t, d), F32)
    n_units = 2 * (tt // CHUNK)
    return pl.pallas_call(
        functools.partial(_wkv_kernel, nt=nt),
        grid=(n_items + 1,),
        in_specs=[fwd] * 6 + [bwd] * 6 + [ka_spec],
        out_specs=[pl.BlockSpec(tile, out_map(False)), pl.BlockSpec(tile, out_map(True))],
        out_shape=[yshape, yshape],
        scratch_shapes=[pltpu.VMEM((HEAD_DIM, gl), F32), pltpu.VMEM((HEAD_DIM, gl), F32),
                        pltpu.VMEM((2, n_units, CHUNK, gl), BF16), pltpu.VMEM((2, n_units, CHUNK, gl), BF16),
                        pltpu.VMEM((2, n_units, CHUNK, gl), F32), pltpu.VMEM((2, n_units, CHUNK, gl), F32)],
        compiler_params=pltpu.CompilerParams(
            dimension_semantics=("arbitrary",), vmem_limit_bytes=VMEM_LIMIT),
        name="wkv",
    )(r, k, v, kk, as0, lw0, r, k, v, kk, as1, lw1, ka)


def _rwkv_post_kernel(yf_ref, yb_ref, r_ref, k_ref, v_ref, as0_ref, as1_ref, z_ref, x_ref,
                      ka_ref, rk_ref, lnw_ref, lnb_ref, wo_ref, pg_ref, ng_ref, win_ref, bin_ref,
                      ones_ref, x1_out, q_out, k_out, v_out, z_out):
    ones_b = ones_ref[...]
    ka = ka_ref[...]
    e = D_MODEL
    y = yf_ref[0] + yb_ref[0]
    mean = _head_sum(y, ones_b, True) * (1.0 / HEAD_DIM)
    yc = y - mean
    var = _head_sum(yc * yc, ones_b, False) * (1.0 / HEAD_DIM)
    yn = yc * lax.rsqrt(var + LNX_EPS) * lnw_ref[...] + lnb_ref[...]
    r = r_ref[0].astype(F32)
    k = k_ref[0].astype(F32)
    v = v_ref[0].astype(F32)
    a_sum = as0_ref[0].astype(F32) + as1_ref[0].astype(F32)
    kd_sum = k * (2.0 + (a_sum - 2.0) * ka)
    bonus = _head_sum(r * kd_sum * rk_ref[...], ones_b, False) * v
    out = (yn + bonus) * _silu(z_ref[0].astype(F32))
    h = _dot(out.astype(BF16), wo_ref[...])
    x1 = x_ref[0] + _rms(h, pg_ref[...])
    x1_out[0] = x1
    xn = _rms(x1, ng_ref[...]).astype(BF16)
    for j, o_ref in enumerate((q_out, k_out, v_out, z_out)):
        pj = _dot(xn, win_ref[:, j * e:(j + 1) * e]) + bin_ref[:, j * e:(j + 1) * e]
        if j == 0:
            pj = pj * (HEAD_DIM ** -0.5 * LOG2E)
        o_ref[0] = pj.astype(BF16)


def _rwkv_post(yf, yb, r, k, v, as0, as1, z, x, ka, rk, lnw, lnb, wo, pg, ng, win, b_in, ones_b):
    bsz, t, d = x.shape
    tt = TIME_TILE
    tile = pl.BlockSpec((1, tt, d), lambda b, i: (b, i, 0))
    consts = (ka, rk, lnw, lnb, wo, pg, ng, win, b_in, ones_b)
    bshape = jax.ShapeDtypeStruct((bsz, t, d), BF16)
    return pl.pallas_call(
        _rwkv_post_kernel,
        grid=(bsz, t // tt),
        in_specs=[tile] * 9 + [_const_spec(a.shape) for a in consts],
        out_specs=[tile] * 5,
        out_shape=[jax.ShapeDtypeStruct((bsz, t, d), F32)] + [bshape] * 4,
        compiler_params=pltpu.CompilerParams(
            dimension_semantics=("parallel", "parallel"), vmem_limit_bytes=VMEM_LIMIT),
        name="rwkv_post",
    )(yf, yb, r, k, v, as0, as1, z, x, *consts)


def _natten_kernel(q_ref, k_ref, v_ref, z_ref, x_ref, bias_ref, wo_ref, bo_ref, pg_ref,
                   out_ref, o_scr, *, rows):
    step = pl.program_id(1)
    gl = LANES_PER_GROUP
    kwin = WIN_ROWS * GRID_W
    mb = _head_lane_masks(GRID_W, BF16)
    mf = _head_lane_masks(GRID_W, F32)

    for j in range(Q_ROWS):
        row = step * Q_ROWS + j
        rs = jnp.clip(row - WIN_ROWS // 2, 0, rows - WIN_ROWS)
        dr0 = rs - row + WIN_ROWS - 1
        k0 = pl.multiple_of(rs * GRID_W, GRID_W)
        qs = slice(j * GRID_W, (j + 1) * GRID_W)
        for g in range(N_GROUPS):
            gs = slice(g * gl, (g + 1) * gl)
            qb = q_ref[0, qs, gs]
            qst = _bd(qb, mb)
            kw = k_ref[0, pl.ds(k0, kwin), gs]
            vw = v_ref[0, pl.ds(k0, kwin), gs]
            s = _dot_nt(qst, kw)
            bias = jnp.concatenate([bias_ref[g, dr0 + 2 * p] for p in range(WIN_ROWS // 2)], axis=-1)
            s = s + bias
            m = jnp.max(s, axis=-1, keepdims=True)
            p = jnp.exp2(s - m)
            l = jnp.sum(p, axis=-1, keepdims=True)
            pv = _dot(p.astype(BF16), vw)
            inv_l = 1.0 / l
            tiles = []
            for lt in range(gl // LANE_TILE):
                acc = None
                for jh in range(HEADS_PER_TILE):
                    hr = slice((lt * HEADS_PER_TILE + jh) * GRID_W, (lt * HEADS_PER_TILE + jh + 1) * GRID_W)
                    blk = pv[hr, lt * LANE_TILE:(lt + 1) * LANE_TILE] * (mf[jh] * inv_l[hr])
                    acc = blk if acc is None else acc + blk
                tiles.append(acc)
            o_scr[qs, gs] = jnp.concatenate(tiles, axis=1)

    gated = o_scr[...] * _silu(z_ref[0].astype(F32))
    h = _dot(gated.astype(BF16), wo_ref[...]) + bo_ref[...]
    out_ref[0] = x_ref[0] + _rms(h, pg_ref[...])


def _natten(q, k, v, z, x1, bias_tab, wo, bo, pg):
    bsz, t, d = x1.shape
    rows = t // GRID_W
    tq = Q_ROWS * GRID_W
    tile = pl.BlockSpec((1, tq, d), lambda b, i: (b, i, 0))
    full = pl.BlockSpec((1, t, d), lambda b, i: (b, 0, 0))
    consts = (bias_tab, wo, bo, pg)
    return pl.pallas_call(
        functools.partial(_natten_kernel, rows=rows),
        grid=(bsz, rows // Q_ROWS),
        in_specs=[tile, full, full, tile, tile] + [_const_spec(a.shape) for a in consts],
        out_specs=tile,
        out_shape=jax.ShapeDtypeStruct((bsz, t, d), F32),
        scratch_shapes=[pltpu.VMEM((tq, d), F32)],
        compiler_params=pltpu.CompilerParams(
            dimension_semantics=("parallel", "arbitrary"), vmem_limit_bytes=VMEM_LIMIT),
        name="natten",
    )(q, k, v, z, x1, *consts)


def _bias_table(rpb):
    qc = np.arange(GRID_W)[:, None]
    kc = np.arange(GRID_W)[None, :]
    ws = np.clip(qc - WIN_COLS // 2, 0, GRID_W - WIN_COLS)
    valid = (kc >= ws) & (kc < ws + WIN_COLS)
    rel = np.clip(kc - qc + WIN_COLS - 1, 0, 2 * WIN_COLS - 2)
    tab = jnp.where(jnp.asarray(valid)[None, None], rpb[:, :, rel] * LOG2E, NEG_INF)
    two = jnp.concatenate([tab[:, :-1], tab[:, 1:]], axis=-1)
    two = two.reshape(N_GROUPS, HEADS_PER_GROUP, 2 * WIN_ROWS - 2, GRID_W, 2 * GRID_W)
    return jnp.transpose(two, (0, 2, 1, 3, 4)).reshape(
        N_GROUPS, 2 * WIN_ROWS - 2, HEADS_PER_GROUP * GRID_W, 2 * GRID_W).astype(F32)


def _trunk(x, p):
    row = lambda a: a.reshape(1, -1)
    r, k, v, z, kk, as0, as1, lw0, lw1 = _rwkv_pre(
        x, row(p["pre_g"][0]), p["mu"], p["w_r"], p["w_k"], p["w_v"], p["w_z"], p["w0"], p["w_down"], p["w2"],
        p["a0"], p["a2"], row(p["k_k"]), p["ones"])
    ka = row(p["k_a"])
    yf, yb = _wkv(r, k, v, kk, as0, lw0, as1, lw1, ka)
    x1, q, k2, v2, z2 = _rwkv_post(
        yf, yb, r, k, v, as0, as1, z, x, ka, row(p["r_k"]), row(p["lnx_w"]), row(p["lnx_b"]), p["rk_w_o"],
        row(p["post_g"][0]), row(p["pre_g"][1]), p["w_in"], row(p["b_in"]), p["ones"])
    return _natten(q, k2, v2, z2, x1, p["bias_tab"], p["na_w_o"], row(p["b_o"]), row(p["post_g"][1]))


def kernel(x_prompt, x_sample, pre_norm_g, post_norm_g, rk_mu, rk_w_r, rk_w_k, rk_w_v, rk_w_z, rk_w0, rk_w1, rk_w2, rk_a0, rk_a1, rk_a2, rk_k_k, rk_k_a, rk_r_k, rk_lnx_w, rk_lnx_b, rk_w_o, na_w_in, na_b_in, na_rpb, na_w_o, na_b_o):
    bf = lambda a: a.astype(BF16)
    hd = np.arange(LANES_PER_GROUP) // HEAD_DIM
    downs = (rk_w1[0, 0], rk_w1[0, 1], rk_a1[0, 0], rk_a1[0, 1])
    lora_down = jnp.concatenate(downs, axis=1)
    lora_down_mu = jnp.concatenate([rk_mu[0, 3 + j][:, None] * w for j, w in enumerate(downs)], axis=1)
    zpad = jnp.zeros((LORA, D_MODEL), F32)
    p = dict(
        pre_g=pre_norm_g, post_g=post_norm_g, mu=rk_mu[0],
        w_r=bf(rk_w_r[0]), w_k=bf(rk_w_k[0]), w_v=bf(rk_w_v[0]), w_z=bf(rk_w_z[0]),
        w0=rk_w0[0], a0=rk_a0[0], w_down=bf(jnp.concatenate([lora_down, lora_down_mu], axis=0)),
        w2=bf(jnp.stack([jnp.concatenate([rk_w2[0, 0], zpad], axis=0), jnp.concatenate([zpad, rk_w2[0, 1]], axis=0)])),
        a2=bf(jnp.stack([jnp.concatenate([rk_a2[0, 0], zpad], axis=0), jnp.concatenate([zpad, rk_a2[0, 1]], axis=0)])),
        k_k=rk_k_k[0], k_a=rk_k_a[0], r_k=rk_r_k[0], lnx_w=rk_lnx_w[0], lnx_b=rk_lnx_b[0],
        rk_w_o=bf(rk_w_o[0]), w_in=bf(na_w_in[0]), b_in=na_b_in[0], na_w_o=bf(na_w_o[0]), b_o=na_b_o[0],
        bias_tab=_bias_table(na_rpb[0]),
        ones=jnp.asarray(hd[:, None] == hd[None, :], dtype=BF16),
    )
    return (_trunk(x_prompt, p), _trunk(x_sample, p))
```
